```python
import math
import jax, jax.numpy as jnp
from jax import lax
import numpy as np

D_MODEL = 1024
BATCH = 16
SEQ = 4096
DEPTH = 4

GRID_W = 64
CTX_LEN = 256
N_MIXERS = 3
CHUNK = 128
EPS = 1e-6
CM_WIDTH = 2 * D_MODEL
CM_GROUPS = 8
LRU_WIDTH = D_MODEL
LRU_HEADS = 4
LRU_BLOCK = LRU_WIDTH // LRU_HEADS
LRU_CONV = 4
LRU_PAD = (2, 1)
LRU_C = 8.0
MLSTM_HEADS = 4
MLSTM_QK = D_MODEL // 2
MLSTM_V = D_MODEL
MLSTM_DK = MLSTM_QK // MLSTM_HEADS
MLSTM_DV = MLSTM_V // MLSTM_HEADS
FFN_HIDDEN = (8 * D_MODEL // 3) // 128 * 128
FFN_CONV = 3
N_CM = (DEPTH + 2) // 3
N_LRU = (DEPTH + 1) // 3
N_ML = DEPTH // 3

kernel_name = "hybrid_chunkmlp_rglru_mlstm_prefix_dit"


def rms_norm(x, g):
    xf = x.astype(jnp.float32)
    y = xf * lax.rsqrt(jnp.mean(xf * xf, axis=-1, keepdims=True) + EPS)
    return (y * g.astype(jnp.float32)).astype(x.dtype)


def layer_norm(x, g, b):
    xf = x.astype(jnp.float32)
    mu = jnp.mean(xf, axis=-1, keepdims=True)
    var = jnp.mean(jnp.square(xf - mu), axis=-1, keepdims=True)
    y = (xf - mu) * lax.rsqrt(var + EPS)
    return (y * g.astype(jnp.float32) + b.astype(jnp.float32)).astype(x.dtype)


def modulate(h, shift, scale):
    return h * (1.0 + scale) + shift


def dwconv1d(x, w, pad):
    return lax.conv_general_dilated(x, w[:, None, :].astype(x.dtype), (1,), [pad],
                                    dimension_numbers=('NWC', 'WIO', 'NWC'),
                                    feature_group_count=x.shape[-1])


def dwconv2d(x, w):
    return lax.conv_general_dilated(x, w[:, :, None, :].astype(x.dtype), (1, 1), [(1, 1), (1, 1)],
                                    dimension_numbers=('NHWC', 'HWIO', 'NHWC'),
                                    feature_group_count=x.shape[-1])


def chunk_mlp(h, w_in, b_in, v_g, v_b, w_s, b_s, w_out):
    bsz, t, _ = h.shape
    z = jax.nn.gelu(h @ w_in + b_in)
    u, v = jnp.split(z, 2, axis=-1)
    v = layer_norm(v, v_g, v_b)
    v = v.reshape(bsz, t // CHUNK, CHUNK, CM_GROUPS, CM_WIDTH // CM_GROUPS)
    s = jnp.einsum('gpq,bnqgc->bnpgc', w_s, v) + jnp.transpose(b_s)[:, :, None]
    return (u * s.reshape(bsz, t, CM_WIDTH)) @ w_out


def _linear_combine(e1, e2):
    a1, b1 = e1
    a2, b2 = e2
    return a1 * a2, a2 * b1 + b2


def rglru_scan(xr, w_rg, b_rg, w_ig, b_ig, lam, h0):
    bsz, t, w = xr.shape
    xh = xr.reshape(bsz, t, LRU_HEADS, LRU_BLOCK)
    r = jax.nn.sigmoid(jnp.einsum('bthi,hij->bthj', xh, w_rg).reshape(bsz, t, w) + b_rg)
    i = jax.nn.sigmoid(jnp.einsum('bthi,hij->bthj', xh, w_ig).reshape(bsz, t, w) + b_ig)
    log_a = -LRU_C * r * jax.nn.softplus(-lam)
    a = jnp.exp(log_a)
    b = jnp.sqrt(-jnp.expm1(2.0 * log_a)) * (i * xr)
    b = b.at[:, 0].add(a[:, 0] * h0)
    _, hs = lax.associative_scan(_linear_combine, (a, b), axis=1)
    return hs, hs[:, -1]


def lru_mixer(hc, hl, w_in, conv_w, conv_b, w_rg, b_rg, w_ig, b_ig, lam, w_out, ctx_out):
    def branches(h):
        y, xr = jnp.split(h @ w_in, 2, axis=-1)
        xr = dwconv1d(xr, conv_w, LRU_PAD) + conv_b
        return jax.nn.gelu(y), xr.astype(jnp.float32)

    yc, xc = branches(hc)
    yl, xl = branches(hl)
    zero = jnp.zeros((hl.shape[0], LRU_WIDTH), jnp.float32)

    def direction(d, xc_d, xl_d):
        p = (w_rg[d], b_rg[d], w_ig[d], b_ig[d], lam[d])
        hc_d, s_ctx = rglru_scan(xc_d, *p, zero)
        hl_d, _ = rglru_scan(xl_d, *p, s_ctx)
        return hc_d, hl_d

    hcf, hlf = direction(0, xc, xl)
    hcb, hlb = direction(1, jnp.flip(xc, 1), jnp.flip(xl, 1))
    ol = ((hlf + jnp.flip(hlb, 1)).astype(hl.dtype) * yl) @ w_out
    oc = ((hcf + jnp.flip(hcb, 1)).astype(hc.dtype) * yc) @ w_out if ctx_out else None
    return oc, ol


def mlstm_chunked(q, k, v, ig, lf, state):
    bsz, nh, t, _ = q.shape
    dv = v.shape[-1]
    nc = t // CHUNK

    def to_chunks(a):
        return jnp.moveaxis(a.reshape(bsz, nh, nc, CHUNK, *a.shape[3:]), 2, 0)

    causal = jnp.tril(jnp.ones((CHUNK, CHUNK), bool))

    def step(carry, xs):
        C, n, m = carry
        qc, kc, vc, ic, fc = xs
        b = jnp.cumsum(fc, axis=-1)
        dlog = jnp.where(causal, b[..., :, None] - b[..., None, :] + ic[..., None, :], -jnp.inf)
        inter = b + m[..., None]
        m_t = jnp.maximum(inter, jnp.max(dlog, axis=-1))
        dw = jnp.exp(dlog - m_t[..., None])
        iw = jnp.exp(inter - m_t)
        s = jnp.einsum('bhtd,bhsd->bhts', qc, kc) * dw
        num = iw[..., None] * jnp.einsum('bhtd,bhde->bhte', qc, C) + jnp.einsum('bhts,bhse->bhte', s, vc)
        den = iw * jnp.einsum('bhtd,bhd->bht', qc, n) + jnp.sum(s, axis=-1)
        h = num / jnp.maximum(jnp.abs(den), jnp.exp(-m_t))[..., None]
        b_end = b[..., -1]
        wlog = b_end[..., None] - b + ic
        m_new = jnp.maximum(b_end + m, jnp.max(wlog, axis=-1))
        decay = jnp.exp(b_end + m - m_new)
        w = jnp.exp(wlog - m_new[..., None])
        C = decay[..., None, None] * C + jnp.einsum('bhs,bhsd,bhse->bhde', w, kc, vc)
        n = decay[..., None] * n + jnp.einsum('bhs,bhsd->bhd', w, kc)
        return (C, n, m_new), h

    state, hs = lax.scan(step, state, (to_chunks(q), to_chunks(k), to_chunks(v), to_chunks(ig), to_chunks(lf)))
    return jnp.moveaxis(hs, 0, 2).reshape(bsz, nh, t, dv), state


def mlstm_mixer(hc, hl, w_in, b_gate, norm_g, w_out, ctx_out):
    splits = [MLSTM_QK, 2 * MLSTM_QK, 2 * MLSTM_QK + MLSTM_V, 2 * MLSTM_QK + 2 * MLSTM_V]

    def project(h):
        bsz, t, _ = h.shape
        q, k, v, o, g = jnp.split(h @ w_in, splits, axis=-1)

        def heads(a, d):
            return jnp.transpose(a.reshape(bsz, t, MLSTM_HEADS, d), (0, 2, 1, 3)).astype(jnp.float32)

        g = g.reshape(bsz, t, 2, 2, MLSTM_HEADS).astype(jnp.float32) + b_gate.astype(jnp.float32)
        g = jnp.transpose(g, (2, 3, 0, 4, 1))
        return heads(q, MLSTM_DK) * MLSTM_DK ** -0.5, heads(k, MLSTM_DK), heads(v, MLSTM_DV), o, g

    qc, kc, vc, oc_g, gc = project(hc)
    ql, kl, vl, ol_g, gl = project(hl)
    bsz = hl.shape[0]
    zero = (jnp.zeros((bsz, MLSTM_HEADS, MLSTM_DK, MLSTM_DV), jnp.float32),
            jnp.zeros((bsz, MLSTM_HEADS, MLSTM_DK), jnp.float32),
            jnp.zeros((bsz, MLSTM_HEADS), jnp.float32))

    def run(d, rev):
        f = (lambda a: jnp.flip(a, axis=2)) if rev else (lambda a: a)
        hc_d, st = mlstm_chunked(f(qc), f(kc), f(vc), f(gc[d, 0]), f(jax.nn.log_sigmoid(gc[d, 1])), zero)
        hl_d, _ = mlstm_chunked(f(ql), f(kl), f(vl), f(gl[d, 0]), f(jax.nn.log_sigmoid(gl[d, 1])), st)
        return f(hc_d), f(hl_d)

    hcf, hlf = run(0, False)
    hcb, hlb = run(1, True)

    def readout(hf, hb, o, dtype):
        h = hf + hb
        h = h * lax.rsqrt(jnp.mean(h * h, axis=-1, keepdims=True) + EPS)
        b_, _, t, _ = h.shape
        h = (jnp.transpose(h, (0, 2, 1, 3)).reshape(b_, t, MLSTM_V) * norm_g).astype(dtype)
        return (jax.nn.sigmoid(o) * h) @ w_out

    ol = readout(hlf, hlb, ol_g, hl.dtype)
    oc = readout(hcf, hcb, oc_g, hc.dtype) if ctx_out else None
    return oc, ol


def conv_ffn(h, w_up, conv_w, conv_b, w_down, on_grid):
    bsz, t, _ = h.shape
    z = h @ w_up
    if on_grid:
        rows = t // GRID_W
        z = dwconv2d(z.reshape(bsz, rows, GRID_W, z.shape[-1]), conv_w).reshape(bsz, t, z.shape[-1])
    else:
        z = dwconv1d(z, conv_w[FFN_CONV // 2], (1, 1))
    g, u = jnp.split(z + conv_b, 2, axis=-1)
    return (jax.nn.silu(g) * u) @ w_down


def setup_inputs(seed: int = 0) -> dict:
    key = jax.random.key(seed)
    ks = iter(jax.random.split(key, 48))
    D = D_MODEL
    F = FFN_HIDDEN

    def nrm(shape, scale):
        return jax.random.normal(next(ks), shape, jnp.float32) * scale

    def gain(shape):
        return 1.0 + nrm(shape, 0.02)

    x = nrm((BATCH, SEQ, D), 1.0)
    c = nrm((BATCH, D), 1.0)
    ctx = nrm((BATCH, CTX_LEN, D), 1.0)
    c_ctx = nrm((D,), 1.0)
    norm1_g = gain((DEPTH, D))
    norm2_g = gain((DEPTH, D))
    mod_w = nrm((DEPTH, D, 6 * D), 0.5 * D ** -0.5)
    mod_b = nrm((DEPTH, 6 * D), 0.02)
    ffn_w_up = nrm((DEPTH, D, 2 * F), D ** -0.5)
    ffn_conv_w = nrm((DEPTH, FFN_CONV, FFN_CONV, 2 * F), 1.0 / FFN_CONV)
    ffn_conv_b = nrm((DEPTH, 2 * F), 0.02)
    ffn_w_down = nrm((DEPTH, F, D), F ** -0.5)
    cm_w_in = nrm((N_CM, D, 2 * CM_WIDTH), D ** -0.5)
    cm_b_in = nrm((N_CM, 2 * CM_WIDTH), 0.02)
    cm_v_g = gain((N_CM, CM_WIDTH))
    cm_v_b = nrm((N_CM, CM_WIDTH), 0.02)
    cm_w_s = nrm((N_CM, CM_GROUPS, CHUNK, CHUNK), CHUNK ** -0.5)
    cm_b_s = 1.0 + nrm((N_CM, CM_GROUPS, CHUNK), 0.02)
    cm_w_out = nrm((N_CM, CM_WIDTH, D), CM_WIDTH ** -0.5)
    lru_w_in = nrm((N_LRU, D, 2 * LRU_WIDTH), D ** -0.5)
    lru_conv_w = nrm((N_LRU, LRU_CONV, LRU_WIDTH), LRU_CONV ** -0.5)
    lru_conv_b = nrm((N_LRU, LRU_WIDTH), 0.02)
    lru_w_rg = nrm((N_LRU, 2, LRU_HEADS, LRU_BLOCK, LRU_BLOCK), LRU_BLOCK ** -0.5)
    lru_b_rg = nrm((N_LRU, 2, LRU_WIDTH), 0.02)
    lru_w_ig = nrm((N_LRU, 2, LRU_HEADS, LRU_BLOCK, LRU_BLOCK), LRU_BLOCK ** -0.5)
    lru_b_ig = nrm((N_LRU, 2, LRU_WIDTH), 0.02)
    p = jax.random.uniform(next(ks), (N_LRU, 2, LRU_WIDTH), jnp.float32, 0.9, 0.999)
    lru_lambda = jnp.log(p) - jnp.log1p(-p)
    lru_w_out = nrm((N_LRU, LRU_WIDTH, D), LRU_WIDTH ** -0.5)
    ml_w_in = nrm((N_ML, D, 2 * MLSTM_QK + 2 * MLSTM_V + 4 * MLSTM_HEADS), D ** -0.5)
    ig_b = nrm((N_ML, 2, 1, MLSTM_HEADS), 0.1)
    fg_b = jax.random.uniform(next(ks), (N_ML, 2, 1, MLSTM_HEADS), jnp.float32, 3.0, 6.0)
    ml_b_gate = jnp.concatenate([ig_b, fg_b], axis=2)
    ml_norm_g = gain((N_ML, MLSTM_V))
    ml_w_out = nrm((N_ML, MLSTM_V, D), MLSTM_V ** -0.5)
    final_norm_g = gain((D,))
    return {"x": x, "c": c, "ctx": ctx, "c_ctx": c_ctx,
            "norm1_g": norm1_g, "norm2_g": norm2_g, "mod_w": mod_w, "mod_b": mod_b,
            "ffn_w_up": ffn_w_up, "ffn_conv_w": ffn_conv_w, "ffn_conv_b": ffn_conv_b, "ffn_w_down": ffn_w_down,
            "cm_w_in": cm_w_in, "cm_b_in": cm_b_in, "cm_v_g": cm_v_g, "cm_v_b": cm_v_b,
            "cm_w_s": cm_w_s, "cm_b_s": cm_b_s, "cm_w_out": cm_w_out,
            "lru_w_in": lru_w_in, "lru_conv_w": lru_conv_w, "lru_conv_b": lru_conv_b,
            "lru_w_rg": lru_w_rg, "lru_b_rg": lru_b_rg, "lru_w_ig": lru_w_ig, "lru_b_ig": lru_b_ig,
            "lru_lambda": lru_lambda, "lru_w_out": lru_w_out,
            "ml_w_in": ml_w_in, "ml_b_gate": ml_b_gate, "ml_norm_g": ml_norm_g, "ml_w_out": ml_w_out,
            "final_norm_g": final_norm_g}


def reference(x, c, ctx, c_ctx, norm1_g, norm2_g, mod_w, mod_b,
              ffn_w_up, ffn_conv_w, ffn_conv_b, ffn_w_down,
              cm_w_in, cm_b_in, cm_v_g, cm_v_b, cm_w_s, cm_b_s, cm_w_out,
              lru_w_in, lru_conv_w, lru_conv_b, lru_w_rg, lru_b_rg, lru_w_ig, lru_b_ig,
              lru_lambda, lru_w_out,
              ml_w_in, ml_b_gate, ml_norm_g, ml_w_out, final_norm_g):
    xl = x
    xc = ctx
    cond_lat = jax.nn.silu(c)
    cond_ctx = jax.nn.silu(c_ctx)
    for l in range(DEPTH):
        kind = l % N_MIXERS
        j = l // N_MIXERS
        last = l == DEPTH - 1
        sh1, sc1, g1, sh2, sc2, g2 = jnp.split((cond_lat @ mod_w[l] + mod_b[l])[:, None, :], 6, axis=-1)
        csh1, csc1, cg1, csh2, csc2, cg2 = jnp.split(cond_ctx @ mod_w[l] + mod_b[l], 6, axis=-1)
        hl = modulate(rms_norm(xl, norm1_g[l]), sh1, sc1)
        if kind == 0:
            cm_args = (cm_w_in[j], cm_b_in[j], cm_v_g[j], cm_v_b[j], cm_w_s[j], cm_b_s[j], cm_w_out[j])
            ol = chunk_mlp(hl, *cm_args)
            oc = None if last else chunk_mlp(modulate(rms_norm(xc, norm1_g[l]), csh1, csc1), *cm_args)
        elif kind == 1:
            hc = modulate(rms_norm(xc, norm1_g[l]), csh1, csc1)
            oc, ol = lru_mixer(hc, hl, lru_w_in[j], lru_conv_w[j], lru_conv_b[j], lru_w_rg[j], lru_b_rg[j],
                               lru_w_ig[j], lru_b_ig[j], lru_lambda[j], lru_w_out[j], not last)
        else:
            hc = modulate(rms_norm(xc, norm1_g[l]), csh1, csc1)
            oc, ol = mlstm_mixer(hc, hl, ml_w_in[j], ml_b_gate[j], ml_norm_g[j], ml_w_out[j], not last)
        ffn_args = (ffn_w_up[l], ffn_conv_w[l], ffn_conv_b[l], ffn_w_down[l])
        xl = xl + g1 * ol
        xl = xl + g2 * conv_ffn(modulate(rms_norm(xl, norm2_g[l]), sh2, sc2), *ffn_args, True)
        if not last:
            xc = xc + cg1 * oc
            xc = xc + cg2 * conv_ffn(modulate(rms_norm(xc, norm2_g[l]), csh2, csc2), *ffn_args, False)
    return rms_norm(xl, final_norm_g)
```

```python
import functools

import jax
import jax.numpy as jnp
from jax import lax
from jax.experimental import pallas as pl
from jax.experimental.pallas import tpu as pltpu

F32 = jnp.float32
BF16 = jnp.bfloat16

D_MODEL = 1024
DEPTH = 4
GRID_W = 64
N_MIXERS = 3
CHUNK = 128
EPS = 1e-6
CM_WIDTH = 2 * D_MODEL
CM_GROUPS = 8
CM_GROUP_W = CM_WIDTH // CM_GROUPS
LRU_WIDTH = D_MODEL
LRU_HEADS = 4
LRU_BLOCK = LRU_WIDTH // LRU_HEADS
LRU_CONV = 4
LRU_C = 8.0
MLSTM_HEADS = 4
MLSTM_QK = D_MODEL // 2
MLSTM_V = D_MODEL
MLSTM_DK = MLSTM_QK // MLSTM_HEADS
MLSTM_DV = MLSTM_V // MLSTM_HEADS
MLSTM_GATES = 4 * MLSTM_HEADS
FFN_HIDDEN = (8 * D_MODEL // 3) // 128 * 128

SUBLANES = 8
LANES = 128
MXU_DIM = 256
VMEM_LIMIT_BYTES = 56 * 1024 * 1024

FFN_CW = MXU_DIM
FFN_HIDDEN_PAD = -(-FFN_HIDDEN // FFN_CW) * FFN_CW
FFN_NC = FFN_HIDDEN_PAD // FFN_CW
FFN_HALO = GRID_W


def _params(n_axes):
    return pltpu.CompilerParams(
        dimension_semantics=("arbitrary",) * n_axes,
        vmem_limit_bytes=VMEM_LIMIT_BYTES,
    )


def _dot(a, b):
    return jnp.dot(a, b, preferred_element_type=F32)


def _dot_nt(a, b):
    return lax.dot_general(a, b, (((1,), (1,)), ((), ())), preferred_element_type=F32)


def _sigmoid(x):
    return 1.0 / (1.0 + jnp.exp(-x))


def _gelu_tanh(x):
    c = 0.7978845608028654
    return 0.5 * x * (1.0 + jnp.tanh(c * (x + 0.044715 * (x * x * x))))


def _softplus(x):
    return jnp.maximum(x, 0.0) + jnp.log1p(jnp.exp(-jnp.abs(x)))


def _rms(x, g):
    return x * lax.rsqrt(jnp.mean(x * x, axis=-1, keepdims=True) + EPS) * g


def _norm_mod(x, g, shift, scale):
    return _rms(x, g) * (1.0 + scale) + shift


def _const_spec(shape):
    n = len(shape)
    return pl.BlockSpec(shape, lambda *_: (0,) * n)


def _resident_spec(shape):
    n = len(shape)
    return pl.BlockSpec(shape, lambda *_: (0,) * n, pipeline_mode=pl.Buffered(1))


def _mod_kernel(c_ref, w_ref, b_ref, o_ref):
    c = c_ref[...]
    s = (c * _sigmoid(c)).astype(BF16)
    o_ref[...] = _dot(s, w_ref[...].astype(BF16)) + b_ref[...]


def _modulation(cond, mod_w, mod_b):
    r = cond.shape[0]
    tn = 6 * D_MODEL // 4
    out = pl.pallas_call(
        _mod_kernel,
        grid=(DEPTH, 6 * D_MODEL // tn),
        in_specs=[
            pl.BlockSpec((r, D_MODEL), lambda l, j: (0, 0)),
            pl.BlockSpec((None, D_MODEL, tn), lambda l, j: (l, 0, j)),
            pl.BlockSpec((None, 1, tn), lambda l, j: (l, 0, j)),
        ],
        out_specs=pl.BlockSpec((None, r, tn), lambda l, j: (l, 0, j)),
        out_shape=jax.ShapeDtypeStruct((DEPTH, r, 6 * D_MODEL), F32),
        compiler_params=_params(2),
        name="modulation",
    )(cond, mod_w, mod_b.reshape(DEPTH, 1, 6 * D_MODEL))
    return out.reshape(DEPTH, r, 6, D_MODEL)


def _cm_kernel(x_ref, m_ref, ng_ref, win_ref, bin_ref, vg_ref, vb_ref, ws_ref, bs_ref, wout_ref,
               o_ref, gate_ref, *, tm):
    x = x_ref[...]
    h = _norm_mod(x, ng_ref[...], m_ref[0:1, :], m_ref[1:2, :]).astype(BF16)
    z = _gelu_tanh(_dot(h, win_ref[...]) + bin_ref[...])
    u = z[:, :CM_WIDTH]
    v = z[:, CM_WIDTH:]
    vc = v - jnp.mean(v, axis=-1, keepdims=True)
    var = jnp.mean(vc * vc, axis=-1, keepdims=True)
    vn = (vc * lax.rsqrt(var + EPS) * vg_ref[...] + vb_ref[...]).astype(BF16)
    for ci in range(tm // CHUNK):
        r0 = ci * CHUNK
        for g in range(CM_GROUPS):
            c0 = g * CM_GROUP_W
            s = _dot(ws_ref[g], vn[r0:r0 + CHUNK, c0:c0 + CM_GROUP_W]) + bs_ref[g]
            gate_ref[r0:r0 + CHUNK, c0:c0 + CM_GROUP_W] = (
                u[r0:r0 + CHUNK, c0:c0 + CM_GROUP_W] * s).astype(BF16)
    o = _dot(gate_ref[...], wout_ref[...])
    o_ref[...] = x + m_ref[2:3, :] * o


def _chunk_mlp_layer(x, mod, mod_row, ng, w_in, b_in, v_g, v_b, w_s, b_s, w_out, tm):
    b, t, d = x.shape
    return pl.pallas_call(
        functools.partial(_cm_kernel, tm=tm),
        grid=(b, t // tm),
        in_specs=[
            pl.BlockSpec((None, tm, d), lambda bi, i: (bi, i, 0)),
            pl.BlockSpec((None, 6, d), lambda bi, i: (mod_row(bi), 0, 0)),
            _const_spec((1, d)),
            _resident_spec((d, 2 * CM_WIDTH)),
            _const_spec((1, 2 * CM_WIDTH)),
            _const_spec((1, CM_WIDTH)),
            _const_spec((1, CM_WIDTH)),
            _const_spec((CM_GROUPS, CHUNK, CHUNK)),
            _const_spec((CM_GROUPS, CHUNK, 1)),
            _resident_spec((CM_WIDTH, d)),
        ],
        out_specs=pl.BlockSpec((None, tm, d), lambda bi, i: (bi, i, 0)),
        out_shape=jax.ShapeDtypeStruct((b, t, d), F32),
        scratch_shapes=[pltpu.VMEM((tm, CM_WIDTH), BF16)],
        compiler_params=_params(2),
        name="chunk_mlp",
    )(x, mod, ng.reshape(1, d), w_in.astype(BF16), b_in.reshape(1, -1), v_g.reshape(1, -1),
      v_b.reshape(1, -1), w_s.astype(BF16), b_s.reshape(CM_GROUPS, CHUNK, 1), w_out.astype(BF16))


def _ffn_kernel(*refs, tm, on_grid, final_norm):
    if final_norm:
        (xm_ref, xp_ref, xn_ref, m_ref, ng_ref, wup_ref, cw_ref, cb_ref, wdn_ref, fg_ref,
         o_ref, h_ref, acc_ref) = refs
    else:
        (xm_ref, xp_ref, xn_ref, m_ref, ng_ref, wup_ref, cw_ref, cb_ref, wdn_ref,
         o_ref, h_ref, acc_ref) = refs
    i = pl.program_id(1)
    c = pl.program_id(2)
    nt = pl.num_programs(1)
    nc = pl.num_programs(2)
    halo = FFN_HALO
    n = tm + 2 * halo

    @pl.when(c == 0)
    def _():
        g, sh, sc = ng_ref[...], m_ref[3:4, :], m_ref[4:5, :]
        h_ref[halo:halo + tm, :] = _norm_mod(xm_ref[...], g, sh, sc).astype(BF16)
        hp = _norm_mod(xp_ref[...], g, sh, sc)
        h_ref[0:halo, :] = jnp.where(i > 0, hp, 0.0).astype(BF16)
        hn = _norm_mod(xn_ref[...], g, sh, sc)
        h_ref[halo + tm:n, :] = jnp.where(i < nt - 1, hn, 0.0).astype(BF16)
        acc_ref[...] = jnp.zeros_like(acc_ref)

    z = _dot(h_ref[...], wup_ref[...])
    zl = pltpu.roll(z, 1, 0)
    zr = pltpu.roll(z, n - 1, 0)
    cwv = cw_ref[...]
    if on_grid:
        wpos = lax.broadcasted_iota(jnp.int32, z.shape, 0) & (GRID_W - 1)
        zl = jnp.where(wpos == 0, 0.0, zl)
        zr = jnp.where(wpos == GRID_W - 1, 0.0, zr)
        conv = None
        for dy in range(3):
            r0 = dy * GRID_W + halo - GRID_W
            part = (cwv[3 * dy:3 * dy + 1, :] * zl[r0:r0 + tm]
                    + cwv[3 * dy + 1:3 * dy + 2, :] * z[r0:r0 + tm]
                    + cwv[3 * dy + 2:3 * dy + 3, :] * zr[r0:r0 + tm])
            conv = part if conv is None else conv + part
    else:
        conv = (cwv[3:4, :] * zl[halo:halo + tm] + cwv[4:5, :] * z[halo:halo + tm]
                + cwv[5:6, :] * zr[halo:halo + tm])
    y = conv + cb_ref[...]
    gg = y[:, :FFN_CW]
    uu = y[:, FFN_CW:]
    a = (gg * _sigmoid(gg) * uu).astype(BF16)
    acc_ref[...] += _dot(a, wdn_ref[...])

    @pl.when(c == nc - 1)
    def _():
        out = xm_ref[...] + m_ref[5:6, :] * acc_ref[...]
        if final_norm:
            out = _rms(out, fg_ref[...])
        o_ref[...] = out


def _ffn_weights(w_up, conv_w, conv_b, w_down):
    f, fp = FFN_HIDDEN, FFN_HIDDEN_PAD

    def pack(a):
        lead = a.shape[:-1]
        a = a.reshape(*lead, 2, f)
        a = jnp.pad(a, [(0, 0)] * len(lead) + [(0, 0), (0, fp - f)])
        a = a.reshape(*lead, 2, FFN_NC, FFN_CW)
        a = jnp.swapaxes(a, -3, -2)
        return a.reshape(*lead, 2 * fp)

    wup = pack(w_up).astype(BF16)
    cw = pack(conv_w.reshape(9, 2 * f))
    cb = pack(conv_b.reshape(1, 2 * f))
    wdn = jnp.pad(w_down, ((0, fp - f), (0, 0))).astype(BF16)
    return wup, cw, cb, wdn


def _ffn_layer(x, mod, mod_row, ng, weights, tm, on_grid, final_g=None):
    b, t, d = x.shape
    wup, cw, cb, wdn = weights
    nt = t // tm
    hb = tm // FFN_HALO
    last_halo = t // FFN_HALO - 1
    final_norm = final_g is not None
    in_specs = [
        pl.BlockSpec((None, tm, d), lambda bi, i, c: (bi, i, 0)),
        pl.BlockSpec((None, FFN_HALO, d), lambda bi, i, c: (bi, jnp.maximum(i * hb - 1, 0), 0)),
        pl.BlockSpec((None, FFN_HALO, d),
                     lambda bi, i, c: (bi, jnp.minimum((i + 1) * hb, last_halo), 0)),
        pl.BlockSpec((None, 6, d), lambda bi, i, c: (mod_row(bi), 0, 0)),
        pl.BlockSpec((1, d), lambda bi, i, c: (0, 0)),
        pl.BlockSpec((d, 2 * FFN_CW), lambda bi, i, c: (0, c)),
        pl.BlockSpec((9, 2 * FFN_CW), lambda bi, i, c: (0, c)),
        pl.BlockSpec((1, 2 * FFN_CW), lambda bi, i, c: (0, c)),
        pl.BlockSpec((FFN_CW, d), lambda bi, i, c: (c, 0)),
    ]
    args = [x, x, x, mod, ng.reshape(1, d), wup, cw, cb, wdn]
    if final_norm:
        in_specs.append(pl.BlockSpec((1, d), lambda bi, i, c: (0, 0)))
        args.append(final_g.reshape(1, d))
    return pl.pallas_call(
        functools.partial(_ffn_kernel, tm=tm, on_grid=on_grid, final_norm=final_norm),
        grid=(b, nt, FFN_NC),
        in_specs=in_specs,
        out_specs=pl.BlockSpec((None, tm, d), lambda bi, i, c: (bi, i, 0)),
        out_shape=jax.ShapeDtypeStruct((b, t, d), F32),
        scratch_shapes=[pltpu.VMEM((tm + 2 * FFN_HALO, d), BF16), pltpu.VMEM((tm, d), F32)],
        compiler_params=_params(3),
        name="conv_ffn_grid" if on_grid else "conv_ffn_seq",
    )(*args)


def _lru_proj_kernel(x_ref, m_ref, ng_ref, win_ref, y_ref, xr_ref):
    h = _norm_mod(x_ref[...], ng_ref[...], m_ref[0:1, :], m_ref[1:2, :]).astype(BF16)
    z = _dot(h, win_ref[...])
    y_ref[...] = _gelu_tanh(z[:, :LRU_WIDTH])
    xr_ref[...] = z[:, LRU_WIDTH:]


def _lru_proj(x, mod, mod_row, ng, w_in, tm):
    b, t, d = x.shape
    w = LRU_WIDTH
    return pl.pallas_call(
        _lru_proj_kernel,
        grid=(b, t // tm),
        in_specs=[
            pl.BlockSpec((None, tm, d), lambda bi, i: (bi, i, 0)),
            pl.BlockSpec((None, 6, d), lambda bi, i: (mod_row(bi), 0, 0)),
            _const_spec((1, d)),
            _const_spec((d, 2 * w)),
        ],
        out_specs=[pl.BlockSpec((None, tm, w), lambda bi, i: (bi, i, 0))] * 2,
        out_shape=[jax.ShapeDtypeStruct((b, t, w), F32)] * 2,
        compiler_params=_params(2),
        name="lru_proj",
    )(x, mod, ng.reshape(1, d), w_in.astype(BF16))


def _lru_scan_kernel(*refs, tm, rev, final):
    if final:
        (xr_ref, xp_ref, xn_ref, cw_ref, cb_ref, wrg_ref, brg_ref, wig_ref, big_ref, lam_ref, h0_ref,
         hb_ref, y_ref, x_ref, m_ref, wout_ref, out_ref, st_ref, carry_ref, a_sc, b_sc, h_sc) = refs
    else:
        (xr_ref, xp_ref, xn_ref, cw_ref, cb_ref, wrg_ref, brg_ref, wig_ref, big_ref, lam_ref, h0_ref,
         out_ref, st_ref, carry_ref, a_sc, b_sc, h_sc) = refs
    i = pl.program_id(1)
    nt = pl.num_programs(1)
    ti = nt - 1 - i if rev else i

    @pl.when(i == 0)
    def _():
        carry_ref[...] = h0_ref[...]

    xm = xr_ref[...]
    row = lax.broadcasted_iota(jnp.int32, xm.shape, 0)
    pv = jnp.where(ti > 0, xp_ref[...], 0.0)
    nx = jnp.where(ti < nt - 1, xn_ref[...], 0.0)
    x_m1 = jnp.where(row == 0, pv[7:8, :], pltpu.roll(xm, 1, 0))
    x_m2 = jnp.where(row == 0, pv[6:7, :], jnp.where(row == 1, pv[7:8, :], pltpu.roll(xm, 2, 0)))
    x_p1 = jnp.where(row == tm - 1, nx[0:1, :], pltpu.roll(xm, tm - 1, 0))
    cw = cw_ref[...]
    xr = (cw[0:1, :] * x_m2 + cw[1:2, :] * x_m1 + cw[2:3, :] * xm + cw[3:4, :] * x_p1) + cb_ref[...]

    xb = xr.astype(BF16)

    def gate(w_ref, b_ref):
        parts = [_dot(xb[:, hh * LRU_BLOCK:(hh + 1) * LRU_BLOCK], w_ref[hh]) for hh in range(LRU_HEADS)]
        return _sigmoid(jnp.concatenate(parts, axis=1) + b_ref[...])

    r = gate(wrg_ref, brg_ref)
    ig = gate(wig_ref, big_ref)
    log_a = (-LRU_C) * r * _softplus(-lam_ref[...])
    a = jnp.exp(log_a)
    bb = jnp.sqrt(-jnp.tanh(log_a) * (a * a + 1.0)) * (ig * xr)

    sub = row & (SUBLANES - 1)
    for s in (1, 2, 4):
        if rev:
            a_s, b_s, ok = pltpu.roll(a, tm - s, 0), pltpu.roll(bb, tm - s, 0), sub < SUBLANES - s
        else:
            a_s, b_s, ok = pltpu.roll(a, s, 0), pltpu.roll(bb, s, 0), sub >= s
        bb = jnp.where(ok, a * b_s + bb, bb)
        a = jnp.where(ok, a * a_s, a)
    a_sc[...] = a
    b_sc[...] = bb

    ng = tm // SUBLANES

    def body(g, carry):
        gi = ng - 1 - g if rev else g
        off = pl.multiple_of(gi * SUBLANES, SUBLANES)
        h8 = a_sc[pl.ds(off, SUBLANES), :] * carry + b_sc[pl.ds(off, SUBLANES), :]
        h_sc[pl.ds(off, SUBLANES), :] = h8
        return h8[0:1, :] if rev else h8[SUBLANES - 1:SUBLANES, :]

    carry = lax.fori_loop(0, ng, body, carry_ref[...])
    carry_ref[...] = carry
    st_ref[...] = carry
    hs = h_sc[...]
    if final:
        mixed = ((hs + hb_ref[...]) * y_ref[...]).astype(BF16)
        out_ref[...] = x_ref[...] + m_ref[2:3, :] * _dot(mixed, wout_ref[...])
    else:
        out_ref[...] = hs


def _lru_scan(xr, h0, params, d, tm, rev, final_args=None):
    b, t, w = xr.shape
    conv_w, conv_b, w_rg, b_rg, w_ig, b_ig, lam = params
    nt = t // tm
    hb = tm // SUBLANES
    last = t // SUBLANES - 1

    def tile(i):
        return nt - 1 - i if rev else i

    in_specs = [
        pl.BlockSpec((None, tm, w), lambda bi, i: (bi, tile(i), 0)),
        pl.BlockSpec((None, SUBLANES, w), lambda bi, i: (bi, jnp.maximum(tile(i) * hb - 1, 0), 0)),
        pl.BlockSpec((None, SUBLANES, w), lambda bi, i: (bi, jnp.minimum((tile(i) + 1) * hb, last), 0)),
        _const_spec((LRU_CONV, w)),
        _const_spec((1, w)),
        _const_spec((LRU_HEADS, LRU_BLOCK, LRU_BLOCK)),
        _const_spec((1, w)),
        _const_spec((LRU_HEADS, LRU_BLOCK, LRU_BLOCK)),
        _const_spec((1, w)),
        _const_spec((1, w)),
        pl.BlockSpec((None, 1, w), lambda bi, i: (bi, 0, 0)),
    ]
    args = [xr, xr, xr, conv_w, conv_b.reshape(1, w), w_rg[d].astype(BF16), b_rg[d].reshape(1, w),
            w_ig[d].astype(BF16), b_ig[d].reshape(1, w), lam[d].reshape(1, w), h0]
    final = final_args is not None
    if final:
        hbk, y, x, mod, mod_row, w_out = final_args
        dm = x.shape[-1]
        in_specs += [
            pl.BlockSpec((None, tm, w), lambda bi, i: (bi, tile(i), 0)),
            pl.BlockSpec((None, tm, w), lambda bi, i: (bi, tile(i), 0)),
            pl.BlockSpec((None, tm, dm), lambda bi, i: (bi, tile(i), 0)),
            pl.BlockSpec((None, 6, dm), lambda bi, i: (mod_row(bi), 0, 0)),
            _const_spec((w, dm)),
        ]
        args += [hbk, y, x, mod, w_out.astype(BF16)]
        out_w = dm
    else:
        out_w = w
    return pl.pallas_call(
        functools.partial(_lru_scan_kernel, tm=tm, rev=rev, final=final),
        grid=(b, nt),
        in_specs=in_specs,
        out_specs=[
            pl.BlockSpec((None, tm, out_w), lambda bi, i: (bi, tile(i), 0)),
            pl.BlockSpec((None, 1, w), lambda bi, i: (bi, 0, 0)),
        ],
        out_shape=[jax.ShapeDtypeStruct((b, t, out_w), F32), jax.ShapeDtypeStruct((b, 1, w), F32)],
        scratch_shapes=[pltpu.VMEM((1, w), F32)] + [pltpu.VMEM((tm, w), F32)] * 3,
        compiler_params=_params(2),
        name="lru_scan_" + ("bwd" if rev else "fwd") + ("_out" if final else ""),
    )(*args)


def _lru_layer(xl, xc, mod, ng, w_in, params, w_out, ctx_out, tm_proj, tm_scan):
    b = xl.shape[0]
    lat_row = lambda bi: bi
    ctx_row = lambda bi: b
    yc, xrc = _lru_proj(xc, mod, ctx_row, ng, w_in, min(tm_proj, xc.shape[1]))
    yl, xrl = _lru_proj(xl, mod, lat_row, ng, w_in, tm_proj)
    zero = jnp.zeros((b, 1, LRU_WIDTH), F32)
    tmc = min(tm_scan, xc.shape[1])
    hbc, sb = _lru_scan(xrc, zero, params, 1, tmc, True)
    hbl, _ = _lru_scan(xrl, sb, params, 1, tm_scan, True)
    if ctx_out:
        xc_new, sf = _lru_scan(xrc, zero, params, 0, tmc, False, (hbc, yc, xc, mod, ctx_row, w_out))
    else:
        xc_new = None
        _, sf = _lru_scan(xrc, zero, params, 0, tmc, False)
    xl_new, _ = _lru_scan(xrl, sf, params, 0, tm_scan, False, (hbl, yl, xl, mod, lat_row, w_out))
    return xc_new, xl_new


def _ml_proj_kernel(x_ref, m_ref, ng_ref, wmain_ref, wgc_ref, wgr_ref, bgc_ref, bgr_ref,
                    q_ref, k_ref, v_ref, o_ref, gc_ref, gr_ref):
    h = _norm_mod(x_ref[...], ng_ref[...], m_ref[0:1, :], m_ref[1:2, :]).astype(BF16)
    z = _dot(h, wmain_ref[...])
    q_ref[...] = (z[:, :MLSTM_QK] * (MLSTM_DK ** -0.5)).astype(BF16)
    k_ref[...] = z[:, MLSTM_QK:2 * MLSTM_QK].astype(BF16)
    v_ref[...] = z[:, 2 * MLSTM_QK:2 * MLSTM_QK + MLSTM_V].astype(BF16)
    o_ref[...] = z[:, 2 * MLSTM_QK + MLSTM_V:]

    def fix(g, idx):
        is_f = (idx & (2 * MLSTM_HEADS - 1)) >= MLSTM_HEADS
        return jnp.where(is_f, -_softplus(-g), g)

    gc = _dot(h, wgc_ref[...]) + bgc_ref[...]
    gc_ref[...] = fix(gc, lax.broadcasted_iota(jnp.int32, gc.shape, 1))
    gr = _dot_nt(wgr_ref[...], h) + bgr_ref[...]
    gr_ref[...] = fix(gr, lax.broadcasted_iota(jnp.int32, gr.shape, 0))


def _ml_proj(x, mod, mod_row, ng, w_in, b_gate, tm):
    b, t, d = x.shape
    nmain = 2 * MLSTM_QK + 2 * MLSTM_V
    wmain = w_in[:, :nmain].astype(BF16)
    wg = w_in[:, nmain:]
    wgc = jnp.pad(wg, ((0, 0), (0, LANES - MLSTM_GATES))).astype(BF16)
    wgr = wg.T.astype(BF16)
    bg = b_gate.reshape(MLSTM_GATES).astype(F32)
    bgc = jnp.pad(bg, (0, LANES - MLSTM_GATES)).reshape(1, LANES)
    bgr = bg.reshape(MLSTM_GATES, 1)
    tok = lambda wd: pl.BlockSpec((None, tm, wd), lambda bi, i: (bi, i, 0))
    return pl.pallas_call(
        _ml_proj_kernel,
        grid=(b, t // tm),
        in_specs=[
            tok(d),
            pl.BlockSpec((None, 6, d), lambda bi, i: (mod_row(bi), 0, 0)),
            _const_spec((1, d)),
            _const_spec((d, nmain)),
            _const_spec((d, LANES)),
            _const_spec((MLSTM_GATES, d)),
            _const_spec((1, LANES)),
            _const_spec((MLSTM_GATES, 1)),
        ],
        out_specs=[tok(MLSTM_QK), tok(MLSTM_QK), tok(MLSTM_V), tok(MLSTM_V), tok(LANES),
                   pl.BlockSpec((None, MLSTM_GATES, tm), lambda bi, i: (bi, 0, i))],
        out_shape=[
            jax.ShapeDtypeStruct((b, t, MLSTM_QK), BF16),
            jax.ShapeDtypeStruct((b, t, MLSTM_QK), BF16),
            jax.ShapeDtypeStruct((b, t, MLSTM_V), BF16),
            jax.ShapeDtypeStruct((b, t, MLSTM_V), F32),
            jax.ShapeDtypeStruct((b, t, LANES), F32),
            jax.ShapeDtypeStruct((b, MLSTM_GATES, t), F32),
        ],
        compiler_params=_params(2),
        name="mlstm_proj",
    )(x, mod, ng.reshape(1, d), wmain, wgc, wgr, bgc, bgr)


def _ml_scan_kernel(q_ref, k_ref, v_ref, gc_ref, gr_ref, c0_ref, n0_ref, m0_ref,
                    h_ref, cst_ref, nst_ref, mst_ref, *, rev):
    i = pl.program_id(1)
    nh, dk, dv = MLSTM_HEADS, MLSTM_DK, MLSTM_DV

    @pl.when(i == 0)
    def _():
        cst_ref[...] = c0_ref[...]
        nst_ref[...] = n0_ref[...]
        mst_ref[...] = m0_ref[...]

    d = 1 if rev else 0
    tt = lax.broadcasted_iota(jnp.int32, (CHUNK, CHUNK), 0)
    ss = lax.broadcasted_iota(jnp.int32, (CHUNK, CHUNK), 1)
    allowed = (ss >= tt) if rev else (ss <= tt)
    tri = allowed.astype(F32)
    gc = gc_ref[...]
    gr = gr_ref[...]
    bc_all = jnp.dot(tri, gc, preferred_element_type=F32, precision=lax.Precision.HIGHEST)
    br_all = lax.dot_general(gr, tri, (((1,), (1,)), ((), ())), preferred_element_type=F32,
                             precision=lax.Precision.HIGHEST)
    outs = []
    for hh in range(nh):
        gi, gf = d * 2 * nh + hh, d * 2 * nh + nh + hh
        ig_col, b_col = gc[:, gi:gi + 1], bc_all[:, gf:gf + 1]
        ig_row, b_row = gr[gi:gi + 1, :], br_all[gf:gf + 1, :]
        b_end = jnp.sum(gr[gf:gf + 1, :], axis=1, keepdims=True)
        m = mst_ref[hh:hh + 1, 0:1]
        cmat = cst_ref[hh]
        nrow = nst_ref[hh:hh + 1, :]
        qh = q_ref[:, hh * dk:(hh + 1) * dk]
        kh = k_ref[:, hh * dk:(hh + 1) * dk]
        vh = v_ref[:, hh * dv:(hh + 1) * dv]

        dlog = jnp.where(allowed, b_col - b_row + ig_row, -jnp.inf)
        inter = b_col + m
        m_t = jnp.maximum(inter, jnp.max(dlog, axis=1, keepdims=True))
        dw = jnp.exp(dlog - m_t)
        iw = jnp.exp(inter - m_t)
        s = _dot_nt(qh, kh) * dw
        num = iw * _dot(qh, cmat.astype(BF16)) + _dot(s.astype(BF16), vh)
        den = iw * jnp.sum(qh.astype(F32) * nrow, axis=1, keepdims=True) + jnp.sum(s, axis=1, keepdims=True)
        outs.append(num / jnp.maximum(jnp.abs(den), jnp.exp(-m_t)))

        wlog_row = b_end - b_row + ig_row
        m_new = jnp.maximum(b_end + m, jnp.max(wlog_row, axis=1, keepdims=True))
        decay = jnp.exp(b_end + m - m_new)
        w_col = jnp.exp((b_end - b_col + ig_col) - m_new)
        kw = kh.astype(F32) * w_col
        cst_ref[hh] = decay * cmat + _dot(kw.T.astype(BF16), vh)
        nst_ref[hh:hh + 1, :] = decay * nrow + jnp.sum(kw, axis=0, keepdims=True)
        mst_ref[hh:hh + 1, :] = jnp.broadcast_to(m_new, (1, LANES))
    h_ref[...] = jnp.concatenate(outs, axis=1)


def _ml_scan(q, k, v, gc, gr, state, rev):
    b, t, _ = q.shape
    nch = t // CHUNK
    c0, n0, m0 = state
    nh, dk, dv = MLSTM_HEADS, MLSTM_DK, MLSTM_DV

    def ch(i):
        return nch - 1 - i if rev else i

    tok = lambda wd: pl.BlockSpec((None, CHUNK, wd), lambda bi, i: (bi, ch(i), 0))
    st_specs = [
        pl.BlockSpec((None, nh, dk, dv), lambda bi, i: (bi, 0, 0, 0)),
        pl.BlockSpec((None, nh, dk), lambda bi, i: (bi, 0, 0)),
        pl.BlockSpec((None, nh, LANES), lambda bi, i: (bi, 0, 0)),
    ]
    outs = pl.pallas_call(
        functools.partial(_ml_scan_kernel, rev=rev),
        grid=(b, nch),
        in_specs=[tok(MLSTM_QK), tok(MLSTM_QK), tok(MLSTM_V), tok(LANES),
                  pl.BlockSpec((None, MLSTM_GATES, CHUNK), lambda bi, i: (bi, 0, ch(i)))] + st_specs,
        out_specs=[tok(MLSTM_V)] + st_specs,
        out_shape=[
            jax.ShapeDtypeStruct((b, t, MLSTM_V), F32),
            jax.ShapeDtypeStruct((b, nh, dk, dv), F32),
            jax.ShapeDtypeStruct((b, nh, dk), F32),
            jax.ShapeDtypeStruct((b, nh, LANES), F32),
        ],
        compiler_params=_params(2),
        name="mlstm_scan_" + ("bwd" if rev else "fwd"),
    )(q, k, v, gc, gr, c0, n0, m0)
    return outs[0], tuple(outs[1:])


def _ml_out_kernel(hf_ref, hb_ref, o_ref, x_ref, m_ref, g_ref, wout_ref, out_ref):
    h = hf_ref[...] + hb_ref[...]
    parts = []
    for hh in range(MLSTM_HEADS):
        hs = h[:, hh * MLSTM_DV:(hh + 1) * MLSTM_DV]
        parts.append(hs * lax.rsqrt(jnp.mean(hs * hs, axis=-1, keepdims=True) + EPS))
    hn = jnp.concatenate(parts, axis=1) * g_ref[...]
    gated = (_sigmoid(o_ref[...]) * hn).astype(BF16)
    out_ref[...] = x_ref[...] + m_ref[2:3, :] * _dot(gated, wout_ref[...])


def _ml_out(hf, hb, o, x, mod, mod_row, norm_g, w_out, tm):
    b, t, d = x.shape
    tok = lambda wd: pl.BlockSpec((None, tm, wd), lambda bi, i: (bi, i, 0))
    return pl.pallas_call(
        _ml_out_kernel,
        grid=(b, t // tm),
        in_specs=[tok(MLSTM_V), tok(MLSTM_V), tok(MLSTM_V), tok(d),
                  pl.BlockSpec((None, 6, d), lambda bi, i: (mod_row(bi), 0, 0)),
                  _const_spec((1, MLSTM_V)), _const_spec((MLSTM_V, d))],
        out_specs=tok(d),
        out_shape=jax.ShapeDtypeStruct((b, t, d), F32),
        compiler_params=_params(2),
        name="mlstm_out",
    )(hf, hb, o, x, mod, norm_g.reshape(1, MLSTM_V), w_out.astype(BF16))


def _mlstm_layer(xl, xc, mod, ng, w_in, b_gate, norm_g, w_out, ctx_out, tm):
    b = xl.shape[0]
    lat_row = lambda bi: bi
    ctx_row = lambda bi: b
    tmc = min(tm, xc.shape[1])
    qc, kc, vc, oc, gcc, grc = _ml_proj(xc, mod, ctx_row, ng, w_in, b_gate, tmc)
    ql, kl, vl, ol, gcl, grl = _ml_proj(xl, mod, lat_row, ng, w_in, b_gate, tm)
    zero = (jnp.zeros((b, MLSTM_HEADS, MLSTM_DK, MLSTM_DV), F32),
            jnp.zeros((b, MLSTM_HEADS, MLSTM_DK), F32),
            jnp.zeros((b, MLSTM_HEADS, LANES), F32))
    hcf, stf = _ml_scan(qc, kc, vc, gcc, grc, zero, False)
    hlf, _ = _ml_scan(ql, kl, vl, gcl, grl, stf, False)
    hcb, stb = _ml_scan(qc, kc, vc, gcc, grc, zero, True)
    hlb, _ = _ml_scan(ql, kl, vl, gcl, grl, stb, True)
    xl_new = _ml_out(hlf, hlb, ol, xl, mod, lat_row, norm_g, w_out, tm)
    xc_new = _ml_out(hcf, hcb, oc, xc, mod, ctx_row, norm_g, w_out, tmc) if ctx_out else None
    return xc_new, xl_new


def _tiles(t_lat):
    return dict(cm=256, ffn=min(1024, t_lat), proj=512, lru_scan=256, ml=512)


def kernel(x, c, ctx, c_ctx, norm1_g, norm2_g, mod_w, mod_b, ffn_w_up, ffn_conv_w, ffn_conv_b, ffn_w_down,
           cm_w_in, cm_b_in, cm_v_g, cm_v_b, cm_w_s, cm_b_s, cm_w_out,
           lru_w_in, lru_conv_w, lru_conv_b, lru_w_rg, lru_b_rg, lru_w_ig, lru_b_ig, lru_lambda, lru_w_out,
           ml_w_in, ml_b_gate, ml_norm_g, ml_w_out, final_norm_g):
    b, t, d = x.shape
    tc = ctx.shape[1]
    assert d == D_MODEL and t % (GRID_W * SUBLANES) == 0 and tc % CHUNK == 0
    ts = _tiles(t)
    rows = -(-(b + 1) // SUBLANES) * SUBLANES
    cond = jnp.concatenate([c, c_ctx[None, :], jnp.zeros((rows - b - 1, d), F32)], axis=0)
    mod_all = _modulation(cond, mod_w, mod_b)
    lat_row = lambda bi: bi
    ctx_row = lambda bi: b

    xl, xc = x, ctx
    for l in range(DEPTH):
        kind, j, last = l % N_MIXERS, l // N_MIXERS, l == DEPTH - 1
        mod = mod_all[l]
        if kind == 0:
            cm = (norm1_g[l], cm_w_in[j], cm_b_in[j], cm_v_g[j], cm_v_b[j], cm_w_s[j], cm_b_s[j], cm_w_out[j])
            xl_mid = _chunk_mlp_layer(xl, mod, lat_row, *cm, tm=ts["cm"])
            xc_mid = None if last else _chunk_mlp_layer(xc, mod, ctx_row, *cm, tm=min(ts["cm"], tc))
        elif kind == 1:
            params = (lru_conv_w[j], lru_conv_b[j], lru_w_rg[j], lru_b_rg[j], lru_w_ig[j], lru_b_ig[j],
                      lru_lambda[j])
            xc_mid, xl_mid = _lru_layer(xl, xc, mod, norm1_g[l], lru_w_in[j], params, lru_w_out[j],
                                        not last, ts["proj"], ts["lru_scan"])
        else:
            xc_mid, xl_mid = _mlstm_layer(xl, xc, mod, norm1_g[l], ml_w_in[j], ml_b_gate[j], ml_norm_g[j],
                                          ml_w_out[j], not last, ts["ml"])
        fw = _ffn_weights(ffn_w_up[l], ffn_conv_w[l], ffn_conv_b[l], ffn_w_down[l])
        xl = _ffn_layer(xl_mid, mod, lat_row, norm2_g[l], fw, ts["ffn"], True,
                        final_norm_g if last else None)
        if not last:
            xc = _ffn_layer(xc_mid, mod, ctx_row, norm2_g[l], fw, min(ts["ffn"], tc), False)
    return xl
```

```python
import functools

import jax
import jax.numpy as jnp
from jax import lax
from jax.experimental import pallas as pl
from jax.experimental.pallas import tpu as pltpu

F32 = jnp.float32
BF16 = jnp.bfloat16

D_MODEL = 1024
DEPTH = 4
GRID_W = 64
N_MIXERS = 3
CHUNK = 128
EPS = 1e-6
CM_WIDTH = 2 * D_MODEL
CM_GROUPS = 8
CM_GROUP_W = CM_WIDTH // CM_GROUPS
LRU_WIDTH = D_MODEL
LRU_HEADS = 4
LRU_BLOCK = LRU_WIDTH // LRU_HEADS
LRU_CONV = 4
LRU_C = 8.0
MLSTM_HEADS = 4
MLSTM_QK = D_MODEL // 2
MLSTM_V = D_MODEL
MLSTM_DK = MLSTM_QK // MLSTM_HEADS
MLSTM_DV = MLSTM_V // MLSTM_HEADS
MLSTM_GATES = 4 * MLSTM_HEADS
FFN_HIDDEN = (8 * D_MODEL // 3) // 128 * 128

SUBLANES = 8
LANES = 128
MXU_DIM = 256
VMEM_LIMIT_BYTES = 56 * 1024 * 1024

FFN_CW = MXU_DIM
FFN_HIDDEN_PAD = -(-FFN_HIDDEN // FFN_CW) * FFN_CW
FFN_NC = FFN_HIDDEN_PAD // FFN_CW
FFN_HALO = GRID_W
FFN_SLICES = 8


def _params(n_axes, flags=None):
    return pltpu.CompilerParams(
        dimension_semantics=("arbitrary",) * n_axes,
        vmem_limit_bytes=VMEM_LIMIT_BYTES,
        flags=flags,
    )


def _dot(a, b):
    return jnp.dot(a, b, preferred_element_type=F32)


def _dot_nt(a, b):
    return lax.dot_general(a, b, (((1,), (1,)), ((), ())), preferred_element_type=F32)


def _sigmoid(x):
    return 1.0 / (1.0 + jnp.exp(-x))


def _gelu_tanh(x):
    c = 0.7978845608028654
    return 0.5 * x * (1.0 + jnp.tanh(c * (x + 0.044715 * (x * x * x))))


def _softplus(x):
    return jnp.maximum(x, 0.0) + jnp.log1p(jnp.exp(-jnp.abs(x)))


def _rms(x, g):
    return x * lax.rsqrt(jnp.mean(x * x, axis=-1, keepdims=True) + EPS) * g


def _norm_mod(x, g, shift, scale):
    return _rms(x, g) * (1.0 + scale) + shift


def _const_spec(shape):
    n = len(shape)
    return pl.BlockSpec(shape, lambda *_: (0,) * n)


def _resident_spec(shape):
    n = len(shape)
    return pl.BlockSpec(shape, lambda *_: (0,) * n, pipeline_mode=pl.Buffered(1))


TOKEN_RUN = GRID_W


def _swap_token_order(x):
    b, t, d = x.shape
    return x.reshape(b, t // TOKEN_RUN, SUBLANES, SUBLANES, d).swapaxes(2, 3).reshape(b, t, d)


def _swap_run_axis(a, axis, runs):
    shp = a.shape
    a = a.reshape(*shp[:axis], runs, SUBLANES, SUBLANES, *shp[axis + 1:])
    return a.swapaxes(axis + 1, axis + 2).reshape(shp)


def _natural_index(p):
    return (p & ~(TOKEN_RUN - 1)) | ((p & (SUBLANES - 1)) << 3) | ((p >> 3) & (SUBLANES - 1))


def _token_shifts(cur, prev_tail, next_head):
    sub = SUBLANES
    first = pltpu.roll(cur[TOKEN_RUN - sub:, :], 1, 0)
    last = pltpu.roll(cur[:sub, :], sub - 1, 0)
    srow = lax.broadcasted_iota(jnp.int32, first.shape, 0)
    if prev_tail is not None:
        first = jnp.where(srow == 0, pltpu.roll(prev_tail, 1, 0), first)
    if next_head is not None:
        last = jnp.where(srow == sub - 1, pltpu.roll(next_head, sub - 1, 0), last)
    left = jnp.concatenate([first, cur[:TOKEN_RUN - sub, :]], axis=0)
    right = jnp.concatenate([cur[sub:, :], last], axis=0)
    return left, right


def _mod_kernel(c_ref, w_ref, b_ref, o_ref):
    c = c_ref[...]
    s = (c * _sigmoid(c)).astype(BF16)
    o_ref[...] = _dot(s, w_ref[...].astype(BF16)) + b_ref[...]


def _modulation(cond, mod_w, mod_b):
    r = cond.shape[0]
    tn = 6 * D_MODEL // 4
    out = pl.pallas_call(
        _mod_kernel,
        grid=(DEPTH, 6 * D_MODEL // tn),
        in_specs=[
            pl.BlockSpec((r, D_MODEL), lambda l, j: (0, 0)),
            pl.BlockSpec((None, D_MODEL, tn), lambda l, j: (l, 0, j)),
            pl.BlockSpec((None, 1, tn), lambda l, j: (l, 0, j)),
        ],
        out_specs=pl.BlockSpec((None, r, tn), lambda l, j: (l, 0, j)),
        out_shape=jax.ShapeDtypeStruct((DEPTH, r, 6 * D_MODEL), F32),
        compiler_params=_params(2),
        name="modulation",
    )(cond, mod_w, mod_b.reshape(DEPTH, 1, 6 * D_MODEL))
    return out.reshape(DEPTH, r, 6, D_MODEL)


def _cm_kernel(x_ref, m_ref, ng_ref, win_ref, bin_ref, vg_ref, vb_ref, ws_ref, bs_ref, wout_ref,
               o_ref, gate_ref, *, tm):
    x = x_ref[...]
    h = _norm_mod(x, ng_ref[...], m_ref[0:1, :], m_ref[1:2, :]).astype(BF16)
    z = _gelu_tanh(_dot(h, win_ref[...]) + bin_ref[...])
    u = z[:, :CM_WIDTH]
    v = z[:, CM_WIDTH:]
    vc = v - jnp.mean(v, axis=-1, keepdims=True)
    var = jnp.mean(vc * vc, axis=-1, keepdims=True)
    vn = (vc * lax.rsqrt(var + EPS) * vg_ref[...] + vb_ref[...]).astype(BF16)
    for ci in range(tm // CHUNK):
        r0 = ci * CHUNK
        for g in range(CM_GROUPS):
            c0 = g * CM_GROUP_W
            s = _dot(ws_ref[g], vn[r0:r0 + CHUNK, c0:c0 + CM_GROUP_W]) + bs_ref[g]
            gate_ref[r0:r0 + CHUNK, c0:c0 + CM_GROUP_W] = (
                u[r0:r0 + CHUNK, c0:c0 + CM_GROUP_W] * s).astype(BF16)
    o = _dot(gate_ref[...], wout_ref[...])
    o_ref[...] = x + m_ref[2:3, :] * o


def _chunk_mlp_layer(x, mod, mod_row, ng, w_in, b_in, v_g, v_b, w_s, b_s, w_out, tm):
    b, t, d = x.shape
    return pl.pallas_call(
        functools.partial(_cm_kernel, tm=tm),
        grid=(b, t // tm),
        in_specs=[
            pl.BlockSpec((None, tm, d), lambda bi, i: (bi, i, 0)),
            pl.BlockSpec((None, 6, d), lambda bi, i: (mod_row(bi), 0, 0)),
            _const_spec((1, d)),
            _resident_spec((d, 2 * CM_WIDTH)),
            _const_spec((1, 2 * CM_WIDTH)),
            _const_spec((1, CM_WIDTH)),
            _const_spec((1, CM_WIDTH)),
            _const_spec((CM_GROUPS, CHUNK, CHUNK)),
            _const_spec((CM_GROUPS, CHUNK, 1)),
            _resident_spec((CM_WIDTH, d)),
        ],
        out_specs=pl.BlockSpec((None, tm, d), lambda bi, i: (bi, i, 0)),
        out_shape=jax.ShapeDtypeStruct((b, t, d), F32),
        scratch_shapes=[pltpu.VMEM((tm, CM_WIDTH), BF16)],
        compiler_params=_params(2),
        name="chunk_mlp",
    )(x, mod, ng.reshape(1, d), w_in.astype(BF16), b_in.reshape(1, -1), v_g.reshape(1, -1),
      v_b.reshape(1, -1), w_s.astype(BF16), b_s.reshape(CM_GROUPS, CHUNK, 1), w_out.astype(BF16))


def _chunk_mlp_stored_order(w_s, b_s):
    runs = CHUNK // TOKEN_RUN
    return _swap_run_axis(_swap_run_axis(w_s, 1, runs), 2, runs), _swap_run_axis(b_s, 1, runs)


def _ffn_kernel(*refs, tm, nt, on_grid, final_norm):
    if final_norm:
        (xm_ref, xp_ref, xn_ref, xl_ref, m_ref, ml_ref, ng_ref, wup_ref, cw_ref, cb_ref, wdn_ref, fg_ref,
         o_ref, h_ref, z0_ref, z1_ref, a0_ref, a1_ref, acc_ref) = refs
    else:
        (xm_ref, xp_ref, xn_ref, xl_ref, m_ref, ml_ref, ng_ref, wup_ref, cw_ref, cb_ref, wdn_ref,
         o_ref, h_ref, z0_ref, z1_ref, a0_ref, a1_ref, acc_ref) = refs
    s = pl.program_id(0)
    nc = FFN_NC
    halo = FFN_HALO
    n = tm + 2 * halo
    c_up = s % nc
    i = (s // nc) % nt
    c_dn = jnp.where(s >= 2, (s - 2) % nc, -1)
    slot = s % 2
    sub = SUBLANES

    @pl.when(s == 0)
    def _():
        for r in (z0_ref, z1_ref, a0_ref, a1_ref, acc_ref):
            r[...] = jnp.zeros_like(r)

    @pl.when(c_up == 0)
    def _():
        g, sh, sc = ng_ref[...], m_ref[3:4, :], m_ref[4:5, :]
        h_ref[halo:halo + tm, :] = _norm_mod(xm_ref[...], g, sh, sc).astype(BF16)
        hp = _norm_mod(xp_ref[...], g, sh, sc)
        h_ref[0:halo, :] = jnp.where(i > 0, hp, 0.0).astype(BF16)
        hn = _norm_mod(xn_ref[...], g, sh, sc)
        h_ref[halo + tm:n, :] = jnp.where(i < nt - 1, hn, 0.0).astype(BF16)

    @pl.when(c_dn == 0)
    def _():
        acc_ref[...] = jnp.zeros_like(acc_ref)

    def stages(zw_ref, zr_ref, aw_ref, ar_ref):
        dys = [((dy - 1) * GRID_W, 3 * dy) for dy in range(3)] if on_grid else [(0, 3)]
        ngr = TOKEN_RUN // sub
        srow = lax.broadcasted_iota(jnp.int32, (sub, LANES), 0)

        def conv_column(rb, c0):
            acc = [jnp.broadcast_to(cb_ref[0:1, c0:c0 + LANES], (sub, LANES))] * ngr
            for off, k in dys:
                tl, tc, tr = [jnp.broadcast_to(cw_ref[k + dx:k + dx + 1, c0:c0 + LANES], (sub, LANES))
                              for dx in range(3)]
                r0 = halo + rb * GRID_W + off
                grp = [zr_ref[r0 + g * sub:r0 + (g + 1) * sub, c0:c0 + LANES] for g in range(ngr)]
                before = pltpu.roll(grp[ngr - 1], 1, 0)
                after = pltpu.roll(grp[0], sub - 1, 0)
                if on_grid:
                    before = jnp.where(srow == 0, 0.0, before)
                    after = jnp.where(srow == sub - 1, 0.0, after)
                else:
                    before = jnp.where(srow == 0, pltpu.roll(zr_ref[r0 - sub:r0, c0:c0 + LANES], 1, 0), before)
                    after = jnp.where(srow == sub - 1, pltpu.roll(
                        zr_ref[r0 + TOKEN_RUN:r0 + TOKEN_RUN + sub, c0:c0 + LANES], sub - 1, 0), after)
                left = [before] + grp[:-1]
                right = grp[1:] + [after]
                acc = [a + (tl * l + tc * c + tr * r) for a, l, c, r in zip(acc, left, grp, right)]
            return acc

        def conv_rows(rb):
            for lb in range(FFN_CW // LANES):
                gates = conv_column(rb, lb * LANES)
                vals = conv_column(rb, FFN_CW + lb * LANES)
                act = [g * _sigmoid(g) * u for g, u in zip(gates, vals)]
                aw_ref[rb * GRID_W:(rb + 1) * GRID_W, lb * LANES:(lb + 1) * LANES] = (
                    jnp.concatenate(act, axis=0).astype(BF16))

        nrb = tm // GRID_W
        nsl = FFN_SLICES if nrb % FFN_SLICES == 0 else nrb
        up_rows, dn_rows = n // nsl, tm // nsl
        for q in range(nsl):
            for rb in range(q * nrb // nsl, (q + 1) * nrb // nsl):
                conv_rows(rb)
            zw_ref[q * up_rows:(q + 1) * up_rows, :] = _dot(h_ref[q * up_rows:(q + 1) * up_rows, :], wup_ref[...])
            acc_ref[q * dn_rows:(q + 1) * dn_rows, :] += _dot(ar_ref[q * dn_rows:(q + 1) * dn_rows, :], wdn_ref[...])

    @pl.when(slot == 0)
    def _():
        stages(z0_ref, z1_ref, a1_ref, a0_ref)

    @pl.when(slot == 1)
    def _():
        stages(z1_ref, z0_ref, a0_ref, a1_ref)

    @pl.when(c_dn == nc - 1)
    def _():
        out = xl_ref[...] + ml_ref[5:6, :] * acc_ref[...]
        if final_norm:
            out = _rms(out, fg_ref[...])
        o_ref[...] = out


def _ffn_weights(w_up, conv_w, conv_b, w_down):
    f, fp = FFN_HIDDEN, FFN_HIDDEN_PAD

    def pack(a):
        lead = a.shape[:-1]
        a = a.reshape(*lead, 2, f)
        a = jnp.pad(a, [(0, 0)] * len(lead) + [(0, 0), (0, fp - f)])
        a = a.reshape(*lead, 2, FFN_NC, FFN_CW)
        a = jnp.swapaxes(a, -3, -2)
        return a.reshape(*lead, 2 * fp)

    wup = pack(w_up).astype(BF16)
    cw = pack(conv_w.reshape(9, 2 * f))
    cb = pack(conv_b.reshape(1, 2 * f))
    wdn = jnp.pad(w_down, ((0, fp - f), (0, 0))).astype(BF16)
    return wup, cw, cb, wdn


def _ffn_layer(x, mod, mod_row, ng, weights, tm, on_grid, final_g=None):
    b, t, d = x.shape
    wup, cw, cb, wdn = weights
    nt = t // tm
    nc = FFN_NC
    hb = tm // FFN_HALO
    last_halo = t // FFN_HALO - 1
    ntiles = b * nt
    final_norm = final_g is not None

    def up_tile(s):
        return jnp.minimum(s // nc, ntiles - 1)

    def dn_tile(s):
        return jnp.maximum(s - 2, 0) // nc

    in_specs = [
        pl.BlockSpec((None, tm, d), lambda s: (up_tile(s) // nt, up_tile(s) % nt, 0)),
        pl.BlockSpec((None, FFN_HALO, d),
                     lambda s: (up_tile(s) // nt, jnp.maximum((up_tile(s) % nt) * hb - 1, 0), 0)),
        pl.BlockSpec((None, FFN_HALO, d),
                     lambda s: (up_tile(s) // nt, jnp.minimum((up_tile(s) % nt + 1) * hb, last_halo), 0)),
        pl.BlockSpec((None, tm, d), lambda s: (dn_tile(s) // nt, dn_tile(s) % nt, 0)),
        pl.BlockSpec((None, 6, d), lambda s: (mod_row(up_tile(s) // nt), 0, 0)),
        pl.BlockSpec((None, 6, d), lambda s: (mod_row(dn_tile(s) // nt), 0, 0)),
        pl.BlockSpec((1, d), lambda s: (0, 0)),
        pl.BlockSpec((d, 2 * FFN_CW), lambda s: (0, s % nc)),
        pl.BlockSpec((9, 2 * FFN_CW), lambda s: (0, jnp.maximum(s - 1, 0) % nc)),
        pl.BlockSpec((1, 2 * FFN_CW), lambda s: (0, jnp.maximum(s - 1, 0) % nc)),
        pl.BlockSpec((FFN_CW, d), lambda s: (jnp.maximum(s - 2, 0) % nc, 0)),
    ]
    args = [x, x, x, x, mod, mod, ng.reshape(1, d), wup, cw, cb, wdn]
    if final_norm:
        in_specs.append(pl.BlockSpec((1, d), lambda s: (0, 0)))
        args.append(final_g.reshape(1, d))
    n = tm + 2 * FFN_HALO
    return pl.pallas_call(
        functools.partial(_ffn_kernel, tm=tm, nt=nt, on_grid=on_grid, final_norm=final_norm),
        grid=(ntiles * nc + 2,),
        in_specs=in_specs,
        out_specs=pl.BlockSpec((None, tm, d), lambda s: (dn_tile(s) // nt, dn_tile(s) % nt, 0)),
        out_shape=jax.ShapeDtypeStruct((b, t, d), F32),
        scratch_shapes=[
            pltpu.VMEM((n, d), BF16),
            pltpu.VMEM((n, 2 * FFN_CW), F32),
            pltpu.VMEM((n, 2 * FFN_CW), F32),
            pltpu.VMEM((tm, FFN_CW), BF16),
            pltpu.VMEM((tm, FFN_CW), BF16),
            pltpu.VMEM((tm, d), F32),
        ],
        compiler_params=_params(1),
        name="conv_ffn_grid" if on_grid else "conv_ffn_seq",
    )(*args)


def _lru_proj_kernel(x_ref, m_ref, ng_ref, win_ref, y_ref, xr_ref):
    h = _norm_mod(x_ref[...], ng_ref[...], m_ref[0:1, :], m_ref[1:2, :]).astype(BF16)
    z = _dot(h, win_ref[...])
    y_ref[...] = _gelu_tanh(z[:, :LRU_WIDTH])
    xr_ref[...] = z[:, LRU_WIDTH:]


def _lru_proj(x, mod, mod_row, ng, w_in, tm):
    b, t, d = x.shape
    w = LRU_WIDTH
    return pl.pallas_call(
        _lru_proj_kernel,
        grid=(b, t // tm),
        in_specs=[
            pl.BlockSpec((None, tm, d), lambda bi, i: (bi, i, 0)),
            pl.BlockSpec((None, 6, d), lambda bi, i: (mod_row(bi), 0, 0)),
            _const_spec((1, d)),
            _const_spec((d, 2 * w)),
        ],
        out_specs=[pl.BlockSpec((None, tm, w), lambda bi, i: (bi, i, 0))] * 2,
        out_shape=[jax.ShapeDtypeStruct((b, t, w), F32)] * 2,
        compiler_params=_params(2),
        name="lru_proj",
    )(x, mod, ng.reshape(1, d), w_in.astype(BF16))


def _lru_scan_kernel(*refs, tm, rev, final):
    if final:
        (xr_ref, xp_ref, xn_ref, cw_ref, cb_ref, wrg_ref, brg_ref, wig_ref, big_ref, lam_ref, h0_ref,
         hb_ref, y_ref, x_ref, m_ref, wout_ref, out_ref, st_ref, carry_ref, xc_sc, r_sc, i_sc, mix_sc) = refs
    else:
        (xr_ref, xp_ref, xn_ref, cw_ref, cb_ref, wrg_ref, brg_ref, wig_ref, big_ref, lam_ref, h0_ref,
         out_ref, st_ref, carry_ref, xc_sc, r_sc, i_sc) = refs
    i = pl.program_id(1)
    nt = pl.num_programs(1)
    ti = nt - 1 - i if rev else i
    sub, ngr = SUBLANES, TOKEN_RUN // SUBLANES
    nruns = tm // TOKEN_RUN
    shape = (sub, LANES)
    srow = lax.broadcasted_iota(jnp.int32, shape, 0)

    @pl.when(i == 0)
    def _():
        carry_ref[...] = jnp.broadcast_to(h0_ref[...], carry_ref.shape)

    def bc(ref, r, c0):
        return jnp.broadcast_to(ref[r:r + 1, c0:c0 + LANES], shape)

    for lb in range(LRU_WIDTH // LANES):
        c0 = lb * LANES
        w0, w1, w2, w3 = [bc(cw_ref, k, c0) for k in range(LRU_CONV)]
        cb = bc(cb_ref, 0, c0)
        for r in range(nruns):
            base = r * TOKEN_RUN
            grp = [xr_ref[base + g * sub:base + (g + 1) * sub, c0:c0 + LANES] for g in range(ngr)]
            if r == 0:
                prev6 = jnp.where(ti > 0, xp_ref[0:sub, c0:c0 + LANES], 0.0)
                prev7 = jnp.where(ti > 0, xp_ref[sub:2 * sub, c0:c0 + LANES], 0.0)
            else:
                prev6 = xr_ref[base - 2 * sub:base - sub, c0:c0 + LANES]
                prev7 = xr_ref[base - sub:base, c0:c0 + LANES]
            if r == nruns - 1:
                next0 = jnp.where(ti < nt - 1, xn_ref[:, c0:c0 + LANES], 0.0)
            else:
                next0 = xr_ref[base + TOKEN_RUN:base + TOKEN_RUN + sub, c0:c0 + LANES]
            before1 = jnp.where(srow == 0, pltpu.roll(prev7, 1, 0), pltpu.roll(grp[ngr - 1], 1, 0))
            before2 = jnp.where(srow == 0, pltpu.roll(prev6, 1, 0), pltpu.roll(grp[ngr - 2], 1, 0))
            after1 = jnp.where(srow == sub - 1, pltpu.roll(next0, sub - 1, 0), pltpu.roll(grp[0], sub - 1, 0))
            m1 = [before1] + grp[:-1]
            m2 = [before2, before1] + grp[:-2]
            p1 = grp[1:] + [after1]
            for g in range(ngr):
                xc_sc[base + g * sub:base + (g + 1) * sub, c0:c0 + LANES] = (
                    w0 * m2[g] + w1 * m1[g] + w2 * grp[g] + w3 * p1[g] + cb)

    for hh in range(LRU_HEADS):
        cols = slice(hh * LRU_BLOCK, (hh + 1) * LRU_BLOCK)
        xb = xc_sc[:, cols].astype(BF16)
        r_sc[:, cols] = _dot(xb, wrg_ref[hh]) + brg_ref[:, cols]
        i_sc[:, cols] = _dot(xb, wig_ref[hh]) + big_ref[:, cols]

    order = list(range(ngr))[::-1] if rev else list(range(ngr))
    shifts = (sub - 1, sub - 2, sub - 4) if rev else (1, 2, 4)
    for lb in range(LRU_WIDTH // LANES):
        c0 = lb * LANES
        sp = jnp.broadcast_to(_softplus(-lam_ref[0:1, c0:c0 + LANES]), shape)
        carry = carry_ref[:, c0:c0 + LANES]
        for r in (range(nruns - 1, -1, -1) if rev else range(nruns)):
            base = r * TOKEN_RUN
            acum, bcum = [None] * ngr, [None] * ngr
            pa = pb = None
            for g in order:
                rows = slice(base + g * sub, base + (g + 1) * sub)
                xv = xc_sc[rows, c0:c0 + LANES]
                log_a = (-LRU_C) * _sigmoid(r_sc[rows, c0:c0 + LANES]) * sp
                a = jnp.exp(log_a)
                bv = jnp.sqrt(-jnp.tanh(log_a) * (a * a + 1.0)) * (_sigmoid(i_sc[rows, c0:c0 + LANES]) * xv)
                if pa is None:
                    pa, pb = a, bv
                else:
                    pa, pb = a * pa, a * pb + bv
                acum[g], bcum[g] = pa, pb
            for k, s in enumerate(shifts):
                step = 1 << k
                ok = (srow < sub - step) if rev else (srow >= step)
                pa_s, pb_s = pltpu.roll(pa, s, 0), pltpu.roll(pb, s, 0)
                pb = jnp.where(ok, pa * pb_s + pb, pb)
                pa = jnp.where(ok, pa * pa_s, pa)
            leave = pa * carry + pb
            if rev:
                enter = jnp.where(srow == sub - 1, carry, pltpu.roll(leave, sub - 1, 0))
                carry = jnp.broadcast_to(leave[0:1, :], shape)
            else:
                enter = jnp.where(srow == 0, carry, pltpu.roll(leave, 1, 0))
                carry = jnp.broadcast_to(leave[sub - 1:sub, :], shape)
            for g in range(ngr):
                rows = slice(base + g * sub, base + (g + 1) * sub)
                hv = acum[g] * enter + bcum[g]
                if final:
                    mix_sc[rows, c0:c0 + LANES] = (hv + hb_ref[rows, c0:c0 + LANES]) * y_ref[rows, c0:c0 + LANES]
                else:
                    out_ref[rows, c0:c0 + LANES] = hv
        carry_ref[:, c0:c0 + LANES] = carry
    st_ref[...] = carry_ref[0:1, :]
    if final:
        out_ref[...] = x_ref[...] + m_ref[2:3, :] * _dot(mix_sc[...].astype(BF16), wout_ref[...])


def _lru_scan(xr, h0, params, d, tm, rev, final_args=None):
    b, t, w = xr.shape
    conv_w, conv_b, w_rg, b_rg, w_ig, b_ig, lam = params
    nt = t // tm
    hb = tm // SUBLANES
    last = t // SUBLANES - 1

    def tile(i):
        return nt - 1 - i if rev else i

    in_specs = [
        pl.BlockSpec((None, tm, w), lambda bi, i: (bi, tile(i), 0)),
        pl.BlockSpec((None, 2 * SUBLANES, w), lambda bi, i: (bi, jnp.maximum(tile(i) * (hb // 2) - 1, 0), 0)),
        pl.BlockSpec((None, SUBLANES, w), lambda bi, i: (bi, jnp.minimum((tile(i) + 1) * hb, last), 0)),
        _const_spec((LRU_CONV, w)),
        _const_spec((1, w)),
        _const_spec((LRU_HEADS, LRU_BLOCK, LRU_BLOCK)),
        _const_spec((1, w)),
        _const_spec((LRU_HEADS, LRU_BLOCK, LRU_BLOCK)),
        _const_spec((1, w)),
        _const_spec((1, w)),
        pl.BlockSpec((None, 1, w), lambda bi, i: (bi, 0, 0)),
    ]
    args = [xr, xr, xr, conv_w, conv_b.reshape(1, w), w_rg[d].astype(BF16), b_rg[d].reshape(1, w),
            w_ig[d].astype(BF16), b_ig[d].reshape(1, w), lam[d].reshape(1, w), h0]
    final = final_args is not None
    if final:
        hbk, y, x, mod, mod_row, w_out = final_args
        dm = x.shape[-1]
        in_specs += [
            pl.BlockSpec((None, tm, w), lambda bi, i: (bi, tile(i), 0)),
            pl.BlockSpec((None, tm, w), lambda bi, i: (bi, tile(i), 0)),
            pl.BlockSpec((None, tm, dm), lambda bi, i: (bi, tile(i), 0)),
            pl.BlockSpec((None, 6, dm), lambda bi, i: (mod_row(bi), 0, 0)),
            _const_spec((w, dm)),
        ]
        args += [hbk, y, x, mod, w_out.astype(BF16)]
        out_w = dm
    else:
        out_w = w
    return pl.pallas_call(
        functools.partial(_lru_scan_kernel, tm=tm, rev=rev, final=final),
        grid=(b, nt),
        in_specs=in_specs,
        out_specs=[
            pl.BlockSpec((None, tm, out_w), lambda bi, i: (bi, tile(i), 0)),
            pl.BlockSpec((None, 1, w), lambda bi, i: (bi, 0, 0)),
        ],
        out_shape=[jax.ShapeDtypeStruct((b, t, out_w), F32), jax.ShapeDtypeStruct((b, 1, w), F32)],
        scratch_shapes=[pltpu.VMEM((SUBLANES, w), F32)] + [pltpu.VMEM((tm, w), F32)] * (4 if final else 3),
        compiler_params=_params(2),
        name="lru_scan_" + ("bwd" if rev else "fwd") + ("_out" if final else ""),
    )(*args)


def _lru_layer(xl, xc, mod, ng, w_in, params, w_out, ctx_out, tm_proj, tm_scan):
    b = xl.shape[0]
    lat_row = lambda bi: bi
    ctx_row = lambda bi: b
    yc, xrc = _lru_proj(xc, mod, ctx_row, ng, w_in, min(tm_proj, xc.shape[1]))
    yl, xrl = _lru_proj(xl, mod, lat_row, ng, w_in, tm_proj)
    zero = jnp.zeros((b, 1, LRU_WIDTH), F32)
    tmc = min(tm_scan, xc.shape[1])
    hbc, sb = _lru_scan(xrc, zero, params, 1, tmc, True)
    hbl, _ = _lru_scan(xrl, sb, params, 1, tm_scan, True)
    if ctx_out:
        xc_new, sf = _lru_scan(xrc, zero, params, 0, tmc, False, (hbc, yc, xc, mod, ctx_row, w_out))
    else:
        xc_new = None
        _, sf = _lru_scan(xrc, zero, params, 0, tmc, False)
    xl_new, _ = _lru_scan(xrl, sf, params, 0, tm_scan, False, (hbl, yl, xl, mod, lat_row, w_out))
    return xc_new, xl_new


def _ml_proj_kernel(x_ref, m_ref, ng_ref, wmain_ref, wgc_ref, wgr_ref, bgc_ref, bgr_ref,
                    q_ref, k_ref, v_ref, o_ref, gc_ref, gr_ref):
    h = _norm_mod(x_ref[...], ng_ref[...], m_ref[0:1, :], m_ref[1:2, :]).astype(BF16)
    z = _dot(h, wmain_ref[...])
    q_ref[...] = (z[:, :MLSTM_QK] * (MLSTM_DK ** -0.5)).astype(BF16)
    k_ref[...] = z[:, MLSTM_QK:2 * MLSTM_QK].astype(BF16)
    v_ref[...] = z[:, 2 * MLSTM_QK:2 * MLSTM_QK + MLSTM_V].astype(BF16)
    o_ref[...] = z[:, 2 * MLSTM_QK + MLSTM_V:]

    def fix(g, idx):
        is_f = (idx & (2 * MLSTM_HEADS - 1)) >= MLSTM_HEADS
        return jnp.where(is_f, -_softplus(-g), g)

    gc = _dot(h, wgc_ref[...]) + bgc_ref[...]
    gc_ref[...] = fix(gc, lax.broadcasted_iota(jnp.int32, gc.shape, 1))
    gr = _dot_nt(wgr_ref[...], h) + bgr_ref[...]
    gr_ref[...] = fix(gr, lax.broadcasted_iota(jnp.int32, gr.shape, 0))


def _ml_proj(x, mod, mod_row, ng, w_in, b_gate, tm):
    b, t, d = x.shape
    nmain = 2 * MLSTM_QK + 2 * MLSTM_V
    wmain = w_in[:, :nmain].astype(BF16)
    wg = w_in[:, nmain:]
    wgc = jnp.pad(wg, ((0, 0), (0, LANES - MLSTM_GATES))).astype(BF16)
    wgr = wg.T.astype(BF16)
    bg = b_gate.reshape(MLSTM_GATES).astype(F32)
    bgc = jnp.pad(bg, (0, LANES - MLSTM_GATES)).reshape(1, LANES)
    bgr = bg.reshape(MLSTM_GATES, 1)
    tok = lambda wd: pl.BlockSpec((None, tm, wd), lambda bi, i: (bi, i, 0))
    return pl.pallas_call(
        _ml_proj_kernel,
        grid=(b, t // tm),
        in_specs=[
            tok(d),
            pl.BlockSpec((None, 6, d), lambda bi, i: (mod_row(bi), 0, 0)),
            _const_spec((1, d)),
            _const_spec((d, nmain)),
            _const_spec((d, LANES)),
            _const_spec((MLSTM_GATES, d)),
            _const_spec((1, LANES)),
            _const_spec((MLSTM_GATES, 1)),
        ],
        out_specs=[tok(MLSTM_QK), tok(MLSTM_QK), tok(MLSTM_V), tok(MLSTM_V), tok(LANES),
                   pl.BlockSpec((None, MLSTM_GATES, tm), lambda bi, i: (bi, 0, i))],
        out_shape=[
            jax.ShapeDtypeStruct((b, t, MLSTM_QK), BF16),
            jax.ShapeDtypeStruct((b, t, MLSTM_QK), BF16),
            jax.ShapeDtypeStruct((b, t, MLSTM_V), BF16),
            jax.ShapeDtypeStruct((b, t, MLSTM_V), F32),
            jax.ShapeDtypeStruct((b, t, LANES), F32),
            jax.ShapeDtypeStruct((b, MLSTM_GATES, t), F32),
        ],
        compiler_params=_params(2),
        name="mlstm_proj",
    )(x, mod, ng.reshape(1, d), wmain, wgc, wgr, bgc, bgr)


def _ml_scan_kernel(q_ref, k_ref, v_ref, gc_ref, gr_ref, c0_ref, n0_ref, m0_ref,
                    h_ref, cst_ref, nst_ref, mst_ref, *, rev):
    i = pl.program_id(1)
    nh, dk, dv = MLSTM_HEADS, MLSTM_DK, MLSTM_DV

    @pl.when(i == 0)
    def _():
        cst_ref[...] = c0_ref[...]
        nst_ref[...] = n0_ref[...]
        mst_ref[...] = m0_ref[...]

    d = 1 if rev else 0
    tt = _natural_index(lax.broadcasted_iota(jnp.int32, (CHUNK, CHUNK), 0))
    ss = _natural_index(lax.broadcasted_iota(jnp.int32, (CHUNK, CHUNK), 1))
    allowed = (ss >= tt) if rev else (ss <= tt)
    tri = allowed.astype(F32)
    gc = gc_ref[...]
    gr = gr_ref[...]
    bc_all = jnp.dot(tri, gc, preferred_element_type=F32, precision=lax.Precision.HIGHEST)
    br_all = lax.dot_general(gr, tri, (((1,), (1,)), ((), ())), preferred_element_type=F32,
                             precision=lax.Precision.HIGHEST)
    outs = []
    for hh in range(nh):
        gi, gf = d * 2 * nh + hh, d * 2 * nh + nh + hh
        ig_col, b_col = gc[:, gi:gi + 1], bc_all[:, gf:gf + 1]
        ig_row, b_row = gr[gi:gi + 1, :], br_all[gf:gf + 1, :]
        b_end = jnp.sum(gr[gf:gf + 1, :], axis=1, keepdims=True)
        m = mst_ref[hh:hh + 1, 0:1]
        cmat = cst_ref[hh]
        nrow = nst_ref[hh:hh + 1, :]
        qh = q_ref[:, hh * dk:(hh + 1) * dk]
        kh = k_ref[:, hh * dk:(hh + 1) * dk]
        vh = v_ref[:, hh * dv:(hh + 1) * dv]

        dlog = jnp.where(allowed, b_col - b_row + ig_row, -jnp.inf)
        inter = b_col + m
        m_t = jnp.maximum(inter, jnp.max(dlog, axis=1, keepdims=True))
        dw = jnp.exp(dlog - m_t)
        iw = jnp.exp(inter - m_t)
        s = _dot_nt(qh, kh) * dw
        num = iw * _dot(qh, cmat.astype(BF16)) + _dot(s.astype(BF16), vh)
        den = iw * jnp.sum(qh.astype(F32) * nrow, axis=1, keepdims=True) + jnp.sum(s, axis=1, keepdims=True)
        outs.append(num / jnp.maximum(jnp.abs(den), jnp.exp(-m_t)))

        wlog_row = b_end - b_row + ig_row
        m_new = jnp.maximum(b_end + m, jnp.max(wlog_row, axis=1, keepdims=True))
        decay = jnp.exp(b_end + m - m_new)
        w_col = jnp.exp((b_end - b_col + ig_col) - m_new)
        kw = kh.astype(F32) * w_col
        cst_ref[hh] = decay * cmat + _dot(kw.T.astype(BF16), vh)
        nst_ref[hh:hh + 1, :] = decay * nrow + jnp.sum(kw, axis=0, keepdims=True)
        mst_ref[hh:hh + 1, :] = jnp.broadcast_to(m_new, (1, LANES))
    h_ref[...] = jnp.concatenate(outs, axis=1)


def _ml_scan(q, k, v, gc, gr, state, rev):
    b, t, _ = q.shape
    nch = t // CHUNK
    c0, n0, m0 = state
    nh, dk, dv = MLSTM_HEADS, MLSTM_DK, MLSTM_DV

    def ch(i):
        return nch - 1 - i if rev else i

    tok = lambda wd: pl.BlockSpec((None, CHUNK, wd), lambda bi, i: (bi, ch(i), 0))
    st_specs = [
        pl.BlockSpec((None, nh, dk, dv), lambda bi, i: (bi, 0, 0, 0)),
        pl.BlockSpec((None, nh, dk), lambda bi, i: (bi, 0, 0)),
        pl.BlockSpec((None, nh, LANES), lambda bi, i: (bi, 0, 0)),
    ]
    outs = pl.pallas_call(
        functools.partial(_ml_scan_kernel, rev=rev),
        grid=(b, nch),
        in_specs=[tok(MLSTM_QK), tok(MLSTM_QK), tok(MLSTM_V), tok(LANES),
                  pl.BlockSpec((None, MLSTM_GATES, CHUNK), lambda bi, i: (bi, 0, ch(i)))] + st_specs,
        out_specs=[tok(MLSTM_V)] + st_specs,
        out_shape=[
            jax.ShapeDtypeStruct((b, t, MLSTM_V), F32),
            jax.ShapeDtypeStruct((b, nh, dk, dv), F32),
            jax.ShapeDtypeStruct((b, nh, dk), F32),
            jax.ShapeDtypeStruct((b, nh, LANES), F32),
        ],
        compiler_params=_params(2),
        name="mlstm_scan_" + ("bwd" if rev else "fwd"),
    )(q, k, v, gc, gr, c0, n0, m0)
    return outs[0], tuple(outs[1:])


def _ml_out_kernel(hf_ref, hb_ref, o_ref, x_ref, m_ref, g_ref, wout_ref, out_ref):
    h = hf_ref[...] + hb_ref[...]
    parts = []
    for hh in range(MLSTM_HEADS):
        hs = h[:, hh * MLSTM_DV:(hh + 1) * MLSTM_DV]
        parts.append(hs * lax.rsqrt(jnp.mean(hs * hs, axis=-1, keepdims=True) + EPS))
    hn = jnp.concatenate(parts, axis=1) * g_ref[...]
    gated = (_sigmoid(o_ref[...]) * hn).astype(BF16)
    out_ref[...] = x_ref[...] + m_ref[2:3, :] * _dot(gated, wout_ref[...])


def _ml_out(hf, hb, o, x, mod, mod_row, norm_g, w_out, tm):
    b, t, d = x.shape
    tok = lambda wd: pl.BlockSpec((None, tm, wd), lambda bi, i: (bi, i, 0))
    return pl.pallas_call(
        _ml_out_kernel,
        grid=(b, t // tm),
        in_specs=[tok(MLSTM_V), tok(MLSTM_V), tok(MLSTM_V), tok(d),
                  pl.BlockSpec((None, 6, d), lambda bi, i: (mod_row(bi), 0, 0)),
                  _const_spec((1, MLSTM_V)), _const_spec((MLSTM_V, d))],
        out_specs=tok(d),
        out_shape=jax.ShapeDtypeStruct((b, t, d), F32),
        compiler_params=_params(2),
        name="mlstm_out",
    )(hf, hb, o, x, mod, norm_g.reshape(1, MLSTM_V), w_out.astype(BF16))


def _mlstm_layer(xl, xc, mod, ng, w_in, b_gate, norm_g, w_out, ctx_out, tm):
    b = xl.shape[0]
    lat_row = lambda bi: bi
    ctx_row = lambda bi: b
    tmc = min(tm, xc.shape[1])
    qc, kc, vc, oc, gcc, grc = _ml_proj(xc, mod, ctx_row, ng, w_in, b_gate, tmc)
    ql, kl, vl, ol, gcl, grl = _ml_proj(xl, mod, lat_row, ng, w_in, b_gate, tm)
    zero = (jnp.zeros((b, MLSTM_HEADS, MLSTM_DK, MLSTM_DV), F32),
            jnp.zeros((b, MLSTM_HEADS, MLSTM_DK), F32),
            jnp.zeros((b, MLSTM_HEADS, LANES), F32))
    hcf, stf = _ml_scan(qc, kc, vc, gcc, grc, zero, False)
    hlf, _ = _ml_scan(ql, kl, vl, gcl, grl, stf, False)
    hcb, stb = _ml_scan(qc, kc, vc, gcc, grc, zero, True)
    hlb, _ = _ml_scan(ql, kl, vl, gcl, grl, stb, True)
    xl_new = _ml_out(hlf, hlb, ol, xl, mod, lat_row, norm_g, w_out, tm)
    xc_new = _ml_out(hcf, hcb, oc, xc, mod, ctx_row, norm_g, w_out, tmc) if ctx_out else None
    return xc_new, xl_new


def _tiles(t_lat):
    return dict(cm=256, ffn=min(1024, t_lat), proj=512, lru_scan=256, ml=512)


def kernel(x, c, ctx, c_ctx, norm1_g, norm2_g, mod_w, mod_b, ffn_w_up, ffn_conv_w, ffn_conv_b, ffn_w_down,
           cm_w_in, cm_b_in, cm_v_g, cm_v_b, cm_w_s, cm_b_s, cm_w_out,
           lru_w_in, lru_conv_w, lru_conv_b, lru_w_rg, lru_b_rg, lru_w_ig, lru_b_ig, lru_lambda, lru_w_out,
           ml_w_in, ml_b_gate, ml_norm_g, ml_w_out, final_norm_g):
    b, t, d = x.shape
    tc = ctx.shape[1]
    assert d == D_MODEL and t % (GRID_W * SUBLANES) == 0 and tc % CHUNK == 0
    ts = _tiles(t)
    rows = -(-(b + 1) // SUBLANES) * SUBLANES
    cond = jnp.concatenate([c, c_ctx[None, :], jnp.zeros((rows - b - 1, d), F32)], axis=0)
    mod_all = _modulation(cond, mod_w, mod_b)
    lat_row = lambda bi: bi
    ctx_row = lambda bi: b

    xl, xc = _swap_token_order(x), _swap_token_order(ctx)
    for l in range(DEPTH):
        kind, j, last = l % N_MIXERS, l // N_MIXERS, l == DEPTH - 1
        mod = mod_all[l]
        if kind == 0:
            w_s, b_s = _chunk_mlp_stored_order(cm_w_s[j], cm_b_s[j])
            cm = (norm1_g[l], cm_w_in[j], cm_b_in[j], cm_v_g[j], cm_v_b[j], w_s, b_s, cm_w_out[j])
            xl_mid = _chunk_mlp_layer(xl, mod, lat_row, *cm, tm=ts["cm"])
            xc_mid = None if last else _chunk_mlp_layer(xc, mod, ctx_row, *cm, tm=min(ts["cm"], tc))
        elif kind == 1:
            params = (lru_conv_w[j], lru_conv_b[j], lru_w_rg[j], lru_b_rg[j], lru_w_ig[j], lru_b_ig[j],
                      lru_lambda[j])
            xc_mid, xl_mid = _lru_layer(xl, xc, mod, norm1_g[l], lru_w_in[j], params, lru_w_out[j],
                                        not last, ts["proj"], ts["lru_scan"])
        else:
            xc_mid, xl_mid = _mlstm_layer(xl, xc, mod, norm1_g[l], ml_w_in[j], ml_b_gate[j], ml_norm_g[j],
                                          ml_w_out[j], not last, ts["ml"])
        fw = _ffn_weights(ffn_w_up[l], ffn_conv_w[l], ffn_conv_b[l], ffn_w_down[l])
        xl = _ffn_layer(xl_mid, mod, lat_row, norm2_g[l], fw, ts["ffn"], True,
                        final_norm_g if last else None)
        if not last:
            xc = _ffn_layer(xc_mid, mod, ctx_row, norm2_g[l], fw, min(ts["ffn"], tc), False)
    return _swap_token_order(xl)
```

```python
import functools

import jax
import jax.numpy as jnp
from jax import lax
from jax.experimental import pallas as pl
from jax.experimental.pallas import tpu as pltpu

F32 = jnp.float32
BF16 = jnp.bfloat16

D_MODEL = 1024
DEPTH = 4
GRID_W = 64
N_MIXERS = 3
CHUNK = 128
EPS = 1e-6
CM_WIDTH = 2 * D_MODEL
CM_GROUPS = 8
CM_GROUP_W = CM_WIDTH // CM_GROUPS
LRU_WIDTH = D_MODEL
LRU_HEADS = 4
LRU_BLOCK = LRU_WIDTH // LRU_HEADS
LRU_CONV = 4
LRU_C = 8.0
MLSTM_HEADS = 4
MLSTM_QK = D_MODEL // 2
MLSTM_V = D_MODEL
MLSTM_DK = MLSTM_QK // MLSTM_HEADS
MLSTM_DV = MLSTM_V // MLSTM_HEADS
MLSTM_GATES = 4 * MLSTM_HEADS
FFN_HIDDEN = (8 * D_MODEL // 3) // 128 * 128

SUBLANES = 8
LANES = 128
MXU_DIM = 256
VMEM_LIMIT_BYTES = 56 * 1024 * 1024

FFN_CW = MXU_DIM
FFN_HIDDEN_PAD = -(-FFN_HIDDEN // FFN_CW) * FFN_CW
FFN_NC = FFN_HIDDEN_PAD // FFN_CW
FFN_HALO = GRID_W
FFN_SLICES = 8


def _params(n_axes, flags=None):
    return pltpu.CompilerParams(
        dimension_semantics=("arbitrary",) * n_axes,
        vmem_limit_bytes=VMEM_LIMIT_BYTES,
        flags=flags,
    )


def _dot(a, b):
    return jnp.dot(a, b, preferred_element_type=F32)


def _dot_nt(a, b):
    return lax.dot_general(a, b, (((1,), (1,)), ((), ())), preferred_element_type=F32)


def _sigmoid(x):
    return 1.0 / (1.0 + jnp.exp(-x))


def _gelu_tanh(x):
    c = 0.7978845608028654
    return 0.5 * x * (1.0 + jnp.tanh(c * (x + 0.044715 * (x * x * x))))


def _softplus(x):
    return jnp.maximum(x, 0.0) + jnp.log1p(jnp.exp(-jnp.abs(x)))


def _rms(x, g):
    return x * lax.rsqrt(jnp.mean(x * x, axis=-1, keepdims=True) + EPS) * g


def _norm_mod(x, g, shift, scale):
    return _rms(x, g) * (1.0 + scale) + shift


def _const_spec(shape):
    n = len(shape)
    return pl.BlockSpec(shape, lambda *_: (0,) * n)


def _resident_spec(shape):
    n = len(shape)
    return pl.BlockSpec(shape, lambda *_: (0,) * n, pipeline_mode=pl.Buffered(1))


TOKEN_RUN = GRID_W


def _swap_token_order(x):
    b, t, d = x.shape
    return x.reshape(b, t // TOKEN_RUN, SUBLANES, SUBLANES, d).swapaxes(2, 3).reshape(b, t, d)


def _swap_run_axis(a, axis, runs):
    shp = a.shape
    a = a.reshape(*shp[:axis], runs, SUBLANES, SUBLANES, *shp[axis + 1:])
    return a.swapaxes(axis + 1, axis + 2).reshape(shp)


def _natural_index(p):
    return (p & ~(TOKEN_RUN - 1)) | ((p & (SUBLANES - 1)) << 3) | ((p >> 3) & (SUBLANES - 1))


def _token_shifts(cur, prev_tail, next_head):
    sub = SUBLANES
    first = pltpu.roll(cur[TOKEN_RUN - sub:, :], 1, 0)
    last = pltpu.roll(cur[:sub, :], sub - 1, 0)
    srow = lax.broadcasted_iota(jnp.int32, first.shape, 0)
    if prev_tail is not None:
        first = jnp.where(srow == 0, pltpu.roll(prev_tail, 1, 0), first)
    if next_head is not None:
        last = jnp.where(srow == sub - 1, pltpu.roll(next_head, sub - 1, 0), last)
    left = jnp.concatenate([first, cur[:TOKEN_RUN - sub, :]], axis=0)
    right = jnp.concatenate([cur[sub:, :], last], axis=0)
    return left, right


def _mod_kernel(c_ref, w_ref, b_ref, o_ref):
    c = c_ref[...]
    s = (c * _sigmoid(c)).astype(BF16)
    o_ref[...] = _dot(s, w_ref[...].astype(BF16)) + b_ref[...]


def _modulation(cond, mod_w, mod_b):
    r = cond.shape[0]
    tn = 6 * D_MODEL // 4
    out = pl.pallas_call(
        _mod_kernel,
        grid=(DEPTH, 6 * D_MODEL // tn),
        in_specs=[
            pl.BlockSpec((r, D_MODEL), lambda l, j: (0, 0)),
            pl.BlockSpec((None, D_MODEL, tn), lambda l, j: (l, 0, j)),
            pl.BlockSpec((None, 1, tn), lambda l, j: (l, 0, j)),
        ],
        out_specs=pl.BlockSpec((None, r, tn), lambda l, j: (l, 0, j)),
        out_shape=jax.ShapeDtypeStruct((DEPTH, r, 6 * D_MODEL), F32),
        compiler_params=_params(2),
        name="modulation",
    )(cond, mod_w, mod_b.reshape(DEPTH, 1, 6 * D_MODEL))
    return out.reshape(DEPTH, r, 6, D_MODEL)


def _cm_kernel(x_ref, m_ref, ng_ref, win_ref, bin_ref, vg_ref, vb_ref, ws_ref, bs_ref, wout_ref,
               o_ref, gate_ref, *, tm):
    x = x_ref[...]
    h = _norm_mod(x, ng_ref[...], m_ref[0:1, :], m_ref[1:2, :]).astype(BF16)
    z = _gelu_tanh(_dot(h, win_ref[...]) + bin_ref[...])
    u = z[:, :CM_WIDTH]
    v = z[:, CM_WIDTH:]
    vc = v - jnp.mean(v, axis=-1, keepdims=True)
    var = jnp.mean(vc * vc, axis=-1, keepdims=True)
    vn = (vc * lax.rsqrt(var + EPS) * vg_ref[...] + vb_ref[...]).astype(BF16)
    for ci in range(tm // CHUNK):
        r0 = ci * CHUNK
        for g in range(CM_GROUPS):
            c0 = g * CM_GROUP_W
            s = _dot(ws_ref[g], vn[r0:r0 + CHUNK, c0:c0 + CM_GROUP_W]) + bs_ref[g]
            gate_ref[r0:r0 + CHUNK, c0:c0 + CM_GROUP_W] = (
                u[r0:r0 + CHUNK, c0:c0 + CM_GROUP_W] * s).astype(BF16)
    o = _dot(gate_ref[...], wout_ref[...])
    o_ref[...] = x + m_ref[2:3, :] * o


def _chunk_mlp_layer(x, mod, mod_row, ng, w_in, b_in, v_g, v_b, w_s, b_s, w_out, tm):
    b, t, d = x.shape
    return pl.pallas_call(
        functools.partial(_cm_kernel, tm=tm),
        grid=(b, t // tm),
        in_specs=[
            pl.BlockSpec((None, tm, d), lambda bi, i: (bi, i, 0)),
            pl.BlockSpec((None, 6, d), lambda bi, i: (mod_row(bi), 0, 0)),
            _const_spec((1, d)),
            _resident_spec((d, 2 * CM_WIDTH)),
            _const_spec((1, 2 * CM_WIDTH)),
            _const_spec((1, CM_WIDTH)),
            _const_spec((1, CM_WIDTH)),
            _const_spec((CM_GROUPS, CHUNK, CHUNK)),
            _const_spec((CM_GROUPS, CHUNK, 1)),
            _resident_spec((CM_WIDTH, d)),
        ],
        out_specs=pl.BlockSpec((None, tm, d), lambda bi, i: (bi, i, 0)),
        out_shape=jax.ShapeDtypeStruct((b, t, d), F32),
        scratch_shapes=[pltpu.VMEM((tm, CM_WIDTH), BF16)],
        compiler_params=_params(2),
        name="chunk_mlp",
    )(x, mod, ng.reshape(1, d), w_in.astype(BF16), b_in.reshape(1, -1), v_g.reshape(1, -1),
      v_b.reshape(1, -1), w_s.astype(BF16), b_s.reshape(CM_GROUPS, CHUNK, 1), w_out.astype(BF16))


def _chunk_mlp_stored_order(w_s, b_s):
    runs = CHUNK // TOKEN_RUN
    return _swap_run_axis(_swap_run_axis(w_s, 1, runs), 2, runs), _swap_run_axis(b_s, 1, runs)


def _ffn_kernel(*refs, tm, nt, on_grid, final_norm):
    if final_norm:
        (xm_ref, xp_ref, xn_ref, xl_ref, m_ref, ml_ref, ng_ref, wup_ref, cw_ref, cb_ref, wdn_ref, fg_ref,
         o_ref, h_ref, z0_ref, z1_ref, a0_ref, a1_ref, acc_ref) = refs
    else:
        (xm_ref, xp_ref, xn_ref, xl_ref, m_ref, ml_ref, ng_ref, wup_ref, cw_ref, cb_ref, wdn_ref,
         o_ref, h_ref, z0_ref, z1_ref, a0_ref, a1_ref, acc_ref) = refs
    s = pl.program_id(0)
    nc = FFN_NC
    halo = FFN_HALO
    n = tm + 2 * halo
    c_up = s % nc
    i = (s // nc) % nt
    c_dn = jnp.where(s >= 2, (s - 2) % nc, -1)
    slot = s % 2
    sub = SUBLANES

    @pl.when(s == 0)
    def _():
        for r in (z0_ref, z1_ref, a0_ref, a1_ref, acc_ref):
            r[...] = jnp.zeros_like(r)

    @pl.when(c_up == 0)
    def _():
        g, sh, sc = ng_ref[...], m_ref[3:4, :], m_ref[4:5, :]
        h_ref[halo:halo + tm, :] = _norm_mod(xm_ref[...], g, sh, sc).astype(BF16)
        hp = _norm_mod(xp_ref[...], g, sh, sc)
        h_ref[0:halo, :] = jnp.where(i > 0, hp, 0.0).astype(BF16)
        hn = _norm_mod(xn_ref[...], g, sh, sc)
        h_ref[halo + tm:n, :] = jnp.where(i < nt - 1, hn, 0.0).astype(BF16)

    @pl.when(c_dn == 0)
    def _():
        acc_ref[...] = jnp.zeros_like(acc_ref)

    def stages(zw_ref, zr_ref, aw_ref, ar_ref):
        dys = [((dy - 1) * GRID_W, 3 * dy) for dy in range(3)] if on_grid else [(0, 3)]
        ngr = TOKEN_RUN // sub
        srow = lax.broadcasted_iota(jnp.int32, (sub, LANES), 0)

        def conv_column(rb, c0):
            acc = [jnp.broadcast_to(cb_ref[0:1, c0:c0 + LANES], (sub, LANES))] * ngr
            for off, k in dys:
                tl, tc, tr = [jnp.broadcast_to(cw_ref[k + dx:k + dx + 1, c0:c0 + LANES], (sub, LANES))
                              for dx in range(3)]
                r0 = halo + rb * GRID_W + off
                grp = [zr_ref[r0 + g * sub:r0 + (g + 1) * sub, c0:c0 + LANES] for g in range(ngr)]
                before = pltpu.roll(grp[ngr - 1], 1, 0)
                after = pltpu.roll(grp[0], sub - 1, 0)
                if on_grid:
                    before = jnp.where(srow == 0, 0.0, before)
                    after = jnp.where(srow == sub - 1, 0.0, after)
                else:
                    before = jnp.where(srow == 0, pltpu.roll(zr_ref[r0 - sub:r0, c0:c0 + LANES], 1, 0), before)
                    after = jnp.where(srow == sub - 1, pltpu.roll(
                        zr_ref[r0 + TOKEN_RUN:r0 + TOKEN_RUN + sub, c0:c0 + LANES], sub - 1, 0), after)
                left = [before] + grp[:-1]
                right = grp[1:] + [after]
                acc = [a + (tl * l + tc * c + tr * r) for a, l, c, r in zip(acc, left, grp, right)]
            return acc

        def conv_rows(rb):
            for lb in range(FFN_CW // LANES):
                gates = conv_column(rb, lb * LANES)
                vals = conv_column(rb, FFN_CW + lb * LANES)
                act = [g * _sigmoid(g) * u for g, u in zip(gates, vals)]
                aw_ref[rb * GRID_W:(rb + 1) * GRID_W, lb * LANES:(lb + 1) * LANES] = (
                    jnp.concatenate(act, axis=0).astype(BF16))

        nrb = tm // GRID_W
        nsl = FFN_SLICES if nrb % FFN_SLICES == 0 else nrb
        up_rows, dn_rows = n // nsl, tm // nsl
        for q in range(nsl):
            for rb in range(q * nrb // nsl, (q + 1) * nrb // nsl):
                conv_rows(rb)
            zw_ref[q * up_rows:(q + 1) * up_rows, :] = _dot(h_ref[q * up_rows:(q + 1) * up_rows, :], wup_ref[...])
            acc_ref[q * dn_rows:(q + 1) * dn_rows, :] += _dot(ar_ref[q * dn_rows:(q + 1) * dn_rows, :], wdn_ref[...])

    @pl.when(slot == 0)
    def _():
        stages(z0_ref, z1_ref, a1_ref, a0_ref)

    @pl.when(slot == 1)
    def _():
        stages(z1_ref, z0_ref, a0_ref, a1_ref)

    @pl.when(c_dn == nc - 1)
    def _():
        out = xl_ref[...] + ml_ref[5:6, :] * acc_ref[...]
        if final_norm:
            out = _rms(out, fg_ref[...])
        o_ref[...] = out


def _ffn_weights(w_up, conv_w, conv_b, w_down):
    f, fp = FFN_HIDDEN, FFN_HIDDEN_PAD
    depth = w_up.shape[0]

    def pack(a):
        lead = a.shape[:-1]
        a = a.reshape(*lead, 2, f)
        a = jnp.pad(a, [(0, 0)] * len(lead) + [(0, 0), (0, fp - f)])
        a = a.reshape(*lead, 2, FFN_NC, FFN_CW)
        a = jnp.swapaxes(a, -3, -2)
        return a.reshape(*lead, FFN_NC, 2 * FFN_CW)

    wup = jnp.swapaxes(pack(w_up.astype(BF16)), 1, 2)
    cw = pack(conv_w.reshape(depth, 9, 2 * f)).reshape(depth, 9, 2 * fp)
    cb = pack(conv_b.reshape(depth, 1, 2 * f)).reshape(depth, 1, 2 * fp)
    wdn = jnp.pad(w_down.astype(BF16), ((0, 0), (0, fp - f), (0, 0)))
    return wup, cw, cb, wdn


def _ffn_layer(x, mod, mod_row, ng, weights, layer, tm, on_grid, final_g=None):
    b, t, d = x.shape
    wup, cw, cb, wdn = weights
    nt = t // tm
    nc = FFN_NC
    hb = tm // FFN_HALO
    last_halo = t // FFN_HALO - 1
    ntiles = b * nt
    final_norm = final_g is not None

    def up_tile(s):
        return jnp.minimum(s // nc, ntiles - 1)

    def dn_tile(s):
        return jnp.maximum(s - 2, 0) // nc

    in_specs = [
        pl.BlockSpec((None, tm, d), lambda s: (up_tile(s) // nt, up_tile(s) % nt, 0)),
        pl.BlockSpec((None, FFN_HALO, d),
                     lambda s: (up_tile(s) // nt, jnp.maximum((up_tile(s) % nt) * hb - 1, 0), 0)),
        pl.BlockSpec((None, FFN_HALO, d),
                     lambda s: (up_tile(s) // nt, jnp.minimum((up_tile(s) % nt + 1) * hb, last_halo), 0)),
        pl.BlockSpec((None, tm, d), lambda s: (dn_tile(s) // nt, dn_tile(s) % nt, 0)),
        pl.BlockSpec((None, 6, d), lambda s: (mod_row(up_tile(s) // nt), 0, 0)),
        pl.BlockSpec((None, 6, d), lambda s: (mod_row(dn_tile(s) // nt), 0, 0)),
        pl.BlockSpec((1, d), lambda s: (0, 0)),
        pl.BlockSpec((None, None, d, 2 * FFN_CW), lambda s: (layer, s % nc, 0, 0)),
        pl.BlockSpec((None, 9, 2 * FFN_CW), lambda s: (layer, 0, jnp.maximum(s - 1, 0) % nc)),
        pl.BlockSpec((None, 1, 2 * FFN_CW), lambda s: (layer, 0, jnp.maximum(s - 1, 0) % nc)),
        pl.BlockSpec((None, FFN_CW, d), lambda s: (layer, jnp.maximum(s - 2, 0) % nc, 0)),
    ]
    args = [x, x, x, x, mod, mod, ng.reshape(1, d), wup, cw, cb, wdn]
    if final_norm:
        in_specs.append(pl.BlockSpec((1, d), lambda s: (0, 0)))
        args.append(final_g.reshape(1, d))
    n = tm + 2 * FFN_HALO
    return pl.pallas_call(
        functools.partial(_ffn_kernel, tm=tm, nt=nt, on_grid=on_grid, final_norm=final_norm),
        grid=(ntiles * nc + 2,),
        in_specs=in_specs,
        out_specs=pl.BlockSpec((None, tm, d), lambda s: (dn_tile(s) // nt, dn_tile(s) % nt, 0)),
        out_shape=jax.ShapeDtypeStruct((b, t, d), F32),
        scratch_shapes=[
            pltpu.VMEM((n, d), BF16),
            pltpu.VMEM((n, 2 * FFN_CW), F32),
            pltpu.VMEM((n, 2 * FFN_CW), F32),
            pltpu.VMEM((tm, FFN_CW), BF16),
            pltpu.VMEM((tm, FFN_CW), BF16),
            pltpu.VMEM((tm, d), F32),
        ],
        compiler_params=_params(1),
        name="conv_ffn_grid" if on_grid else "conv_ffn_seq",
    )(*args)


def _lru_proj_kernel(x_ref, m_ref, ng_ref, win_ref, y_ref, xr_ref):
    h = _norm_mod(x_ref[...], ng_ref[...], m_ref[0:1, :], m_ref[1:2, :]).astype(BF16)
    z = _dot(h, win_ref[...])
    y_ref[...] = _gelu_tanh(z[:, :LRU_WIDTH])
    xr_ref[...] = z[:, LRU_WIDTH:]


def _lru_proj(x, mod, mod_row, ng, w_in, tm):
    b, t, d = x.shape
    w = LRU_WIDTH
    return pl.pallas_call(
        _lru_proj_kernel,
        grid=(b, t // tm),
        in_specs=[
            pl.BlockSpec((None, tm, d), lambda bi, i: (bi, i, 0)),
            pl.BlockSpec((None, 6, d), lambda bi, i: (mod_row(bi), 0, 0)),
            _const_spec((1, d)),
            _const_spec((d, 2 * w)),
        ],
        out_specs=[pl.BlockSpec((None, tm, w), lambda bi, i: (bi, i, 0))] * 2,
        out_shape=[jax.ShapeDtypeStruct((b, t, w), F32)] * 2,
        compiler_params=_params(2),
        name="lru_proj",
    )(x, mod, ng.reshape(1, d), w_in.astype(BF16))


def _lru_scan_kernel(*refs, tm, rev, final):
    if final:
        (xr_ref, xp_ref, xn_ref, cw_ref, cb_ref, wrg_ref, brg_ref, wig_ref, big_ref, lam_ref, h0_ref,
         hb_ref, y_ref, x_ref, m_ref, wout_ref, out_ref, st_ref, carry_ref, xc_sc, r_sc, i_sc, mix_sc) = refs
    else:
        (xr_ref, xp_ref, xn_ref, cw_ref, cb_ref, wrg_ref, brg_ref, wig_ref, big_ref, lam_ref, h0_ref,
         out_ref, st_ref, carry_ref, xc_sc, r_sc, i_sc) = refs
    i = pl.program_id(1)
    nt = pl.num_programs(1)
    ti = nt - 1 - i if rev else i
    sub, ngr = SUBLANES, TOKEN_RUN // SUBLANES
    nruns = tm // TOKEN_RUN
    shape = (sub, LANES)
    srow = lax.broadcasted_iota(jnp.int32, shape, 0)

    @pl.when(i == 0)
    def _():
        carry_ref[...] = jnp.broadcast_to(h0_ref[...], carry_ref.shape)

    def bc(ref, r, c0):
        return jnp.broadcast_to(ref[r:r + 1, c0:c0 + LANES], shape)

    for lb in range(LRU_WIDTH // LANES):
        c0 = lb * LANES
        w0, w1, w2, w3 = [bc(cw_ref, k, c0) for k in range(LRU_CONV)]
        cb = bc(cb_ref, 0, c0)
        for r in range(nruns):
            base = r * TOKEN_RUN
            grp = [xr_ref[base + g * sub:base + (g + 1) * sub, c0:c0 + LANES] for g in range(ngr)]
            if r == 0:
                prev6 = jnp.where(ti > 0, xp_ref[0:sub, c0:c0 + LANES], 0.0)
                prev7 = jnp.where(ti > 0, xp_ref[sub:2 * sub, c0:c0 + LANES], 0.0)
            else:
                prev6 = xr_ref[base - 2 * sub:base - sub, c0:c0 + LANES]
                prev7 = xr_ref[base - sub:base, c0:c0 + LANES]
            if r == nruns - 1:
                next0 = jnp.where(ti < nt - 1, xn_ref[:, c0:c0 + LANES], 0.0)
            else:
                next0 = xr_ref[base + TOKEN_RUN:base + TOKEN_RUN + sub, c0:c0 + LANES]
            before1 = jnp.where(srow == 0, pltpu.roll(prev7, 1, 0), pltpu.roll(grp[ngr - 1], 1, 0))
            before2 = jnp.where(srow == 0, pltpu.roll(prev6, 1, 0), pltpu.roll(grp[ngr - 2], 1, 0))
            after1 = jnp.where(srow == sub - 1, pltpu.roll(next0, sub - 1, 0), pltpu.roll(grp[0], sub - 1, 0))
            m1 = [before1] + grp[:-1]
            m2 = [before2, before1] + grp[:-2]
            p1 = grp[1:] + [after1]
            for g in range(ngr):
                xc_sc[base + g * sub:base + (g + 1) * sub, c0:c0 + LANES] = (
                    w0 * m2[g] + w1 * m1[g] + w2 * grp[g] + w3 * p1[g] + cb)

    for hh in range(LRU_HEADS):
        cols = slice(hh * LRU_BLOCK, (hh + 1) * LRU_BLOCK)
        xb = xc_sc[:, cols].astype(BF16)
        r_sc[:, cols] = _dot(xb, wrg_ref[hh]) + brg_ref[:, cols]
        i_sc[:, cols] = _dot(xb, wig_ref[hh]) + big_ref[:, cols]

    order = list(range(ngr))[::-1] if rev else list(range(ngr))
    shifts = (sub - 1, sub - 2, sub - 4) if rev else (1, 2, 4)
    for lb in range(LRU_WIDTH // LANES):
        c0 = lb * LANES
        sp = jnp.broadcast_to(_softplus(-lam_ref[0:1, c0:c0 + LANES]), shape)
        carry = carry_ref[:, c0:c0 + LANES]
        for r in (range(nruns - 1, -1, -1) if rev else range(nruns)):
            base = r * TOKEN_RUN
            acum, bcum = [None] * ngr, [None] * ngr
            pa = pb = None
            for g in order:
                rows = slice(base + g * sub, base + (g + 1) * sub)
                xv = xc_sc[rows, c0:c0 + LANES]
                log_a = (-LRU_C) * _sigmoid(r_sc[rows, c0:c0 + LANES]) * sp
                a = jnp.exp(log_a)
                bv = jnp.sqrt(-jnp.tanh(log_a) * (a * a + 1.0)) * (_sigmoid(i_sc[rows, c0:c0 + LANES]) * xv)
                if pa is None:
                    pa, pb = a, bv
                else:
                    pa, pb = a * pa, a * pb + bv
                acum[g], bcum[g] = pa, pb
            for k, s in enumerate(shifts):
                step = 1 << k
                ok = (srow < sub - step) if rev else (srow >= step)
                pa_s, pb_s = pltpu.roll(pa, s, 0), pltpu.roll(pb, s, 0)
                pb = jnp.where(ok, pa * pb_s + pb, pb)
                pa = jnp.where(ok, pa * pa_s, pa)
            leave = pa * carry + pb
            if rev:
                enter = jnp.where(srow == sub - 1, carry, pltpu.roll(leave, sub - 1, 0))
                carry = jnp.broadcast_to(leave[0:1, :], shape)
            else:
                enter = jnp.where(srow == 0, carry, pltpu.roll(leave, 1, 0))
                carry = jnp.broadcast_to(leave[sub - 1:sub, :], shape)
            for g in range(ngr):
                rows = slice(base + g * sub, base + (g + 1) * sub)
                hv = acum[g] * enter + bcum[g]
                if final:
                    mix_sc[rows, c0:c0 + LANES] = (hv + hb_ref[rows, c0:c0 + LANES]) * y_ref[rows, c0:c0 + LANES]
                else:
                    out_ref[rows, c0:c0 + LANES] = hv
        carry_ref[:, c0:c0 + LANES] = carry
    st_ref[...] = carry_ref[0:1, :]
    if final:
        out_ref[...] = x_ref[...] + m_ref[2:3, :] * _dot(mix_sc[...].astype(BF16), wout_ref[...])


def _lru_scan(xr, h0, params, d, tm, rev, final_args=None):
    b, t, w = xr.shape
    conv_w, conv_b, w_rg, b_rg, w_ig, b_ig, lam = params
    nt = t // tm
    hb = tm // SUBLANES
    last = t // SUBLANES - 1

    def tile(i):
        return nt - 1 - i if rev else i

    in_specs = [
        pl.BlockSpec((None, tm, w), lambda bi, i: (bi, tile(i), 0)),
        pl.BlockSpec((None, 2 * SUBLANES, w), lambda bi, i: (bi, jnp.maximum(tile(i) * (hb // 2) - 1, 0), 0)),
        pl.BlockSpec((None, SUBLANES, w), lambda bi, i: (bi, jnp.minimum((tile(i) + 1) * hb, last), 0)),
        _const_spec((LRU_CONV, w)),
        _const_spec((1, w)),
        _const_spec((LRU_HEADS, LRU_BLOCK, LRU_BLOCK)),
        _const_spec((1, w)),
        _const_spec((LRU_HEADS, LRU_BLOCK, LRU_BLOCK)),
        _const_spec((1, w)),
        _const_spec((1, w)),
        pl.BlockSpec((None, 1, w), lambda bi, i: (bi, 0, 0)),
    ]
    args = [xr, xr, xr, conv_w, conv_b.reshape(1, w), w_rg[d].astype(BF16), b_rg[d].reshape(1, w),
            w_ig[d].astype(BF16), b_ig[d].reshape(1, w), lam[d].reshape(1, w), h0]
    final = final_args is not None
    if final:
        hbk, y, x, mod, mod_row, w_out = final_args
        dm = x.shape[-1]
        in_specs += [
            pl.BlockSpec((None, tm, w), lambda bi, i: (bi, tile(i), 0)),
            pl.BlockSpec((None, tm, w), lambda bi, i: (bi, tile(i), 0)),
            pl.BlockSpec((None, tm, dm), lambda bi, i: (bi, tile(i), 0)),
            pl.BlockSpec((None, 6, dm), lambda bi, i: (mod_row(bi), 0, 0)),
            _const_spec((w, dm)),
        ]
        args += [hbk, y, x, mod, w_out.astype(BF16)]
        out_w = dm
    else:
        out_w = w
    return pl.pallas_call(
        functools.partial(_lru_scan_kernel, tm=tm, rev=rev, final=final),
        grid=(b, nt),
        in_specs=in_specs,
        out_specs=[
            pl.BlockSpec((None, tm, out_w), lambda bi, i: (bi, tile(i), 0)),
            pl.BlockSpec((None, 1, w), lambda bi, i: (bi, 0, 0)),
        ],
        out_shape=[jax.ShapeDtypeStruct((b, t, out_w), F32), jax.ShapeDtypeStruct((b, 1, w), F32)],
        scratch_shapes=[pltpu.VMEM((SUBLANES, w), F32)] + [pltpu.VMEM((tm, w), F32)] * (4 if final else 3),
        compiler_params=_params(2),
        name="lru_scan_" + ("bwd" if rev else "fwd") + ("_out" if final else ""),
    )(*args)


def _lru_layer(xl, xc, mod, ng, w_in, params, w_out, ctx_out, tm_proj, tm_scan):
    b = xl.shape[0]
    lat_row = lambda bi: bi
    ctx_row = lambda bi: b
    yc, xrc = _lru_proj(xc, mod, ctx_row, ng, w_in, min(tm_proj, xc.shape[1]))
    yl, xrl = _lru_proj(xl, mod, lat_row, ng, w_in, tm_proj)
    zero = jnp.zeros((b, 1, LRU_WIDTH), F32)
    tmc = min(tm_scan, xc.shape[1])
    hbc, sb = _lru_scan(xrc, zero, params, 1, tmc, True)
    hbl, _ = _lru_scan(xrl, sb, params, 1, tm_scan, True)
    if ctx_out:
        xc_new, sf = _lru_scan(xrc, zero, params, 0, tmc, False, (hbc, yc, xc, mod, ctx_row, w_out))
    else:
        xc_new = None
        _, sf = _lru_scan(xrc, zero, params, 0, tmc, False)
    xl_new, _ = _lru_scan(xrl, sf, params, 0, tm_scan, False, (hbl, yl, xl, mod, lat_row, w_out))
    return xc_new, xl_new


def _ml_proj_kernel(x_ref, m_ref, ng_ref, wmain_ref, wgc_ref, wgr_ref, bgc_ref, bgr_ref,
                    q_ref, k_ref, v_ref, o_ref, gc_ref, gr_ref):
    h = _norm_mod(x_ref[...], ng_ref[...], m_ref[0:1, :], m_ref[1:2, :]).astype(BF16)
    z = _dot(h, wmain_ref[...])
    q_ref[...] = (z[:, :MLSTM_QK] * (MLSTM_DK ** -0.5)).astype(BF16)
    k_ref[...] = z[:, MLSTM_QK:2 * MLSTM_QK].astype(BF16)
    v_ref[...] = z[:, 2 * MLSTM_QK:2 * MLSTM_QK + MLSTM_V].astype(BF16)
    o_ref[...] = z[:, 2 * MLSTM_QK + MLSTM_V:]

    def fix(g, idx):
        is_f = (idx & (2 * MLSTM_HEADS - 1)) >= MLSTM_HEADS
        return jnp.where(is_f, -_softplus(-g), g)

    gc = _dot(h, wgc_ref[...]) + bgc_ref[...]
    gc_ref[...] = fix(gc, lax.broadcasted_iota(jnp.int32, gc.shape, 1))
    gr = _dot_nt(wgr_ref[...], h) + bgr_ref[...]
    gr_ref[...] = fix(gr, lax.broadcasted_iota(jnp.int32, gr.shape, 0))


def _ml_proj(x, mod, mod_row, ng, w_in, b_gate, tm):
    b, t, d = x.shape
    nmain = 2 * MLSTM_QK + 2 * MLSTM_V
    wmain = w_in[:, :nmain].astype(BF16)
    wg = w_in[:, nmain:]
    wgc = jnp.pad(wg, ((0, 0), (0, LANES - MLSTM_GATES))).astype(BF16)
    wgr = wg.T.astype(BF16)
    bg = b_gate.reshape(MLSTM_GATES).astype(F32)
    bgc = jnp.pad(bg, (0, LANES - MLSTM_GATES)).reshape(1, LANES)
    bgr = bg.reshape(MLSTM_GATES, 1)
    tok = lambda wd: pl.BlockSpec((None, tm, wd), lambda bi, i: (bi, i, 0))
    return pl.pallas_call(
        _ml_proj_kernel,
        grid=(b, t // tm),
        in_specs=[
            tok(d),
            pl.BlockSpec((None, 6, d), lambda bi, i: (mod_row(bi), 0, 0)),
            _const_spec((1, d)),
            _const_spec((d, nmain)),
            _const_spec((d, LANES)),
            _const_spec((MLSTM_GATES, d)),
            _const_spec((1, LANES)),
            _const_spec((MLSTM_GATES, 1)),
        ],
        out_specs=[tok(MLSTM_QK), tok(MLSTM_QK), tok(MLSTM_V), tok(MLSTM_V), tok(LANES),
                   pl.BlockSpec((None, MLSTM_GATES, tm), lambda bi, i: (bi, 0, i))],
        out_shape=[
            jax.ShapeDtypeStruct((b, t, MLSTM_QK), BF16),
            jax.ShapeDtypeStruct((b, t, MLSTM_QK), BF16),
            jax.ShapeDtypeStruct((b, t, MLSTM_V), BF16),
            jax.ShapeDtypeStruct((b, t, MLSTM_V), F32),
            jax.ShapeDtypeStruct((b, t, LANES), F32),
            jax.ShapeDtypeStruct((b, MLSTM_GATES, t), F32),
        ],
        compiler_params=_params(2),
        name="mlstm_proj",
    )(x, mod, ng.reshape(1, d), wmain, wgc, wgr, bgc, bgr)


def _ml_chunk_setup(refs, rev):
    q_ref, k_ref, v_ref, gc_ref, gr_ref, c0_ref, n0_ref, m0_ref, h_ref, cst_ref, nst_ref, mst_ref = refs

    @pl.when(pl.program_id(1) == 0)
    def _():
        cst_ref[...] = c0_ref[...]
        nst_ref[...] = n0_ref[...]
        mst_ref[...] = m0_ref[...]

    tt = _natural_index(lax.broadcasted_iota(jnp.int32, (CHUNK, CHUNK), 0))
    ss = _natural_index(lax.broadcasted_iota(jnp.int32, (CHUNK, CHUNK), 1))
    allowed = (ss >= tt) if rev else (ss <= tt)
    tri = allowed.astype(F32)
    gc = gc_ref[...]
    gr = gr_ref[...]
    bc_all = jnp.dot(tri, gc, preferred_element_type=F32, precision=lax.Precision.HIGHEST)
    br_all = lax.dot_general(gr, tri, (((1,), (1,)), ((), ())), preferred_element_type=F32,
                             precision=lax.Precision.HIGHEST)
    return allowed, gc, gr, bc_all, br_all


def _ml_chunk_head(refs, setup, rev, hh):
    q_ref, k_ref, v_ref, gc_ref, gr_ref, c0_ref, n0_ref, m0_ref, h_ref, cst_ref, nst_ref, mst_ref = refs
    allowed, gc, gr, bc_all, br_all = setup
    nh, dk, dv = MLSTM_HEADS, MLSTM_DK, MLSTM_DV
    d = 1 if rev else 0
    gi, gf = d * 2 * nh + hh, d * 2 * nh + nh + hh
    ig_col, b_col = gc[:, gi:gi + 1], bc_all[:, gf:gf + 1]
    ig_row, b_row = gr[gi:gi + 1, :], br_all[gf:gf + 1, :]
    b_end = jnp.sum(gr[gf:gf + 1, :], axis=1, keepdims=True)
    m = mst_ref[hh:hh + 1, 0:1]
    cmat = cst_ref[hh]
    nrow = nst_ref[hh:hh + 1, :]
    qh = q_ref[:, hh * dk:(hh + 1) * dk]
    kh = k_ref[:, hh * dk:(hh + 1) * dk]
    vh = v_ref[:, hh * dv:(hh + 1) * dv]

    qk = _dot_nt(qh, kh)
    qc = _dot(qh, cmat.astype(BF16))
    dlog = jnp.where(allowed, b_col - b_row + ig_row, -jnp.inf)
    inter = b_col + m
    wlog_row = b_end - b_row + ig_row
    yield
    m_t = jnp.maximum(inter, jnp.max(dlog, axis=1, keepdims=True))
    m_new = jnp.maximum(b_end + m, jnp.max(wlog_row, axis=1, keepdims=True))
    yield
    dw = jnp.exp(dlog - m_t)
    iw = jnp.exp(inter - m_t)
    decay = jnp.exp(b_end + m - m_new)
    w_col = jnp.exp((b_end - b_col + ig_col) - m_new)
    yield
    s = qk * dw
    kw = kh.astype(F32) * w_col
    yield
    sv = _dot(s.astype(BF16), vh)
    kv = _dot(kw.T.astype(BF16), vh)
    den = iw * jnp.sum(qh.astype(F32) * nrow, axis=1, keepdims=True) + jnp.sum(s, axis=1, keepdims=True)
    n_new = decay * nrow + jnp.sum(kw, axis=0, keepdims=True)
    yield
    h_out = (iw * qc + sv) / jnp.maximum(jnp.abs(den), jnp.exp(-m_t))
    c_new = decay * cmat + kv
    yield
    h_ref[:, hh * dv:(hh + 1) * dv] = h_out
    cst_ref[hh] = c_new
    nst_ref[hh:hh + 1, :] = n_new
    mst_ref[hh:hh + 1, :] = jnp.broadcast_to(m_new, (1, LANES))


def _ml_scan_kernel(*refs):
    fwd = refs[0:8] + refs[16:20]
    bwd = refs[8:16] + refs[20:24]
    setup_f = _ml_chunk_setup(fwd, False)
    setup_b = _ml_chunk_setup(bwd, True)
    heads = []
    for hh in range(MLSTM_HEADS):
        heads.append(_ml_chunk_head(fwd, setup_f, False, hh))
        heads.append(_ml_chunk_head(bwd, setup_b, True, hh))
    while heads:
        alive = []
        for gen in heads:
            try:
                next(gen)
                alive.append(gen)
            except StopIteration:
                pass
        heads = alive


def _ml_scan(q, k, v, gc, gr, state_f, state_b):
    b, t, _ = q.shape
    nch = t // CHUNK
    nh, dk, dv = MLSTM_HEADS, MLSTM_DK, MLSTM_DV
    st_specs = [
        pl.BlockSpec((None, nh, dk, dv), lambda bi, i: (bi, 0, 0, 0)),
        pl.BlockSpec((None, nh, dk), lambda bi, i: (bi, 0, 0)),
        pl.BlockSpec((None, nh, LANES), lambda bi, i: (bi, 0, 0)),
    ]
    st_shapes = [
        jax.ShapeDtypeStruct((b, nh, dk, dv), F32),
        jax.ShapeDtypeStruct((b, nh, dk), F32),
        jax.ShapeDtypeStruct((b, nh, LANES), F32),
    ]

    def dir_specs(ch):
        tok = lambda wd: pl.BlockSpec((None, CHUNK, wd), lambda bi, i: (bi, ch(i), 0))
        ins = [tok(MLSTM_QK), tok(MLSTM_QK), tok(MLSTM_V), tok(LANES),
               pl.BlockSpec((None, MLSTM_GATES, CHUNK), lambda bi, i: (bi, 0, ch(i)))] + st_specs
        return ins, [tok(MLSTM_V)] + st_specs

    in_f, out_f = dir_specs(lambda i: i)
    in_b, out_b = dir_specs(lambda i: nch - 1 - i)
    h_shape = jax.ShapeDtypeStruct((b, t, MLSTM_V), F32)
    outs = pl.pallas_call(
        _ml_scan_kernel,
        grid=(b, nch),
        in_specs=in_f + in_b,
        out_specs=out_f + out_b,
        out_shape=[h_shape] + st_shapes + [h_shape] + st_shapes,
        compiler_params=_params(2),
        name="mlstm_scan",
    )(q, k, v, gc, gr, *state_f, q, k, v, gc, gr, *state_b)
    return outs[0], outs[4], tuple(outs[1:4]), tuple(outs[5:8])


def _ml_out_kernel(hf_ref, hb_ref, o_ref, x_ref, m_ref, g_ref, wout_ref, out_ref):
    h = hf_ref[...] + hb_ref[...]
    parts = []
    for hh in range(MLSTM_HEADS):
        hs = h[:, hh * MLSTM_DV:(hh + 1) * MLSTM_DV]
        parts.append(hs * lax.rsqrt(jnp.mean(hs * hs, axis=-1, keepdims=True) + EPS))
    hn = jnp.concatenate(parts, axis=1) * g_ref[...]
    gated = (_sigmoid(o_ref[...]) * hn).astype(BF16)
    out_ref[...] = x_ref[...] + m_ref[2:3, :] * _dot(gated, wout_ref[...])


def _ml_out(hf, hb, o, x, mod, mod_row, norm_g, w_out, tm):
    b, t, d = x.shape
    tok = lambda wd: pl.BlockSpec((None, tm, wd), lambda bi, i: (bi, i, 0))
    return pl.pallas_call(
        _ml_out_kernel,
        grid=(b, t // tm),
        in_specs=[tok(MLSTM_V), tok(MLSTM_V), tok(MLSTM_V), tok(d),
                  pl.BlockSpec((None, 6, d), lambda bi, i: (mod_row(bi), 0, 0)),
                  _const_spec((1, MLSTM_V)), _const_spec((MLSTM_V, d))],
        out_specs=tok(d),
        out_shape=jax.ShapeDtypeStruct((b, t, d), F32),
        compiler_params=_params(2),
        name="mlstm_out",
    )(hf, hb, o, x, mod, norm_g.reshape(1, MLSTM_V), w_out.astype(BF16))


def _mlstm_layer(xl, xc, mod, ng, w_in, b_gate, norm_g, w_out, ctx_out, tm):
    b = xl.shape[0]
    lat_row = lambda bi: bi
    ctx_row = lambda bi: b
    tmc = min(tm, xc.shape[1])
    qc, kc, vc, oc, gcc, grc = _ml_proj(xc, mod, ctx_row, ng, w_in, b_gate, tmc)
    ql, kl, vl, ol, gcl, grl = _ml_proj(xl, mod, lat_row, ng, w_in, b_gate, tm)
    zero = (jnp.zeros((b, MLSTM_HEADS, MLSTM_DK, MLSTM_DV), F32),
            jnp.zeros((b, MLSTM_HEADS, MLSTM_DK), F32),
            jnp.zeros((b, MLSTM_HEADS, LANES), F32))
    hcf, hcb, stf, stb = _ml_scan(qc, kc, vc, gcc, grc, zero, zero)
    hlf, hlb, _, _ = _ml_scan(ql, kl, vl, gcl, grl, stf, stb)
    xl_new = _ml_out(hlf, hlb, ol, xl, mod, lat_row, norm_g, w_out, tm)
    xc_new = _ml_out(hcf, hcb, oc, xc, mod, ctx_row, norm_g, w_out, tmc) if ctx_out else None
    return xc_new, xl_new


def _tiles(t_lat):
    return dict(cm=256, ffn=min(1024, t_lat), proj=512, lru_scan=256, ml=512)


def kernel(x, c, ctx, c_ctx, norm1_g, norm2_g, mod_w, mod_b, ffn_w_up, ffn_conv_w, ffn_conv_b, ffn_w_down,
           cm_w_in, cm_b_in, cm_v_g, cm_v_b, cm_w_s, cm_b_s, cm_w_out,
           lru_w_in, lru_conv_w, lru_conv_b, lru_w_rg, lru_b_rg, lru_w_ig, lru_b_ig, lru_lambda, lru_w_out,
           ml_w_in, ml_b_gate, ml_norm_g, ml_w_out, final_norm_g):
    b, t, d = x.shape
    tc = ctx.shape[1]
    assert d == D_MODEL and t % (GRID_W * SUBLANES) == 0 and tc % CHUNK == 0
    ts = _tiles(t)
    rows = -(-(b + 1) // SUBLANES) * SUBLANES
    cond = jnp.concatenate([c, c_ctx[None, :], jnp.zeros((rows - b - 1, d), F32)], axis=0)
    mod_all = _modulation(cond, mod_w, mod_b)
    lat_row = lambda bi: bi
    ctx_row = lambda bi: b

    fw = _ffn_weights(ffn_w_up, ffn_conv_w, ffn_conv_b, ffn_w_down)
    xl, xc = _swap_token_order(x), _swap_token_order(ctx)
    for l in range(DEPTH):
        kind, j, last = l % N_MIXERS, l // N_MIXERS, l == DEPTH - 1
        mod = mod_all[l]
        if kind == 0:
            w_s, b_s = _chunk_mlp_stored_order(cm_w_s[j], cm_b_s[j])
            cm = (norm1_g[l], cm_w_in[j], cm_b_in[j], cm_v_g[j], cm_v_b[j], w_s, b_s, cm_w_out[j])
            xl_mid = _chunk_mlp_layer(xl, mod, lat_row, *cm, tm=ts["cm"])
            xc_mid = None if last else _chunk_mlp_layer(xc, mod, ctx_row, *cm, tm=min(ts["cm"], tc))
        elif kind == 1:
            params = (lru_conv_w[j], lru_conv_b[j], lru_w_rg[j], lru_b_rg[j], lru_w_ig[j], lru_b_ig[j],
                      lru_lambda[j])
            xc_mid, xl_mid = _lru_layer(xl, xc, mod, norm1_g[l], lru_w_in[j], params, lru_w_out[j],
                                        not last, ts["proj"], ts["lru_scan"])
        else:
            xc_mid, xl_mid = _mlstm_layer(xl, xc, mod, norm1_g[l], ml_w_in[j], ml_b_gate[j], ml_norm_g[j],
                                          ml_w_out[j], not last, ts["ml"])
        xl = _ffn_layer(xl_mid, mod, lat_row, norm2_g[l], fw, l, ts["ffn"], True,
                        final_norm_g if last else None)
        if not last:
            xc = _ffn_layer(xc_mid, mod, ctx_row, norm2_g[l], fw, l, min(ts["ffn"], tc), False)
    return _swap_token_order(xl)
```
